```python
import jax
import jax.numpy as jnp
from jax import lax
import numpy as np

D_MODEL = 1024
BATCH = 16
SEQ = 2048
DEPTH = 1

GDN_HEADS = 8
GDN_DK = 128
GDN_DV = 128
GDN_CONV = 4
CHUNK = 64
SC_WIDTH = D_MODEL
SC_CONV = 3
D_FF_RAW = (8 * D_MODEL + 2) // 3
D_FF = (D_FF_RAW + 255) // 256 * 256
EPS = 1e-6
N_MOD = 6

QK_W = GDN_HEADS * GDN_DK
V_W = GDN_HEADS * GDN_DV
QKV_W = 2 * QK_W + V_W
Q_OFF = 0
K_OFF = Q_OFF + QK_W
V_OFF = K_OFF + QK_W
Z_OFF = V_OFF + V_W
A_OFF = Z_OFF + V_W
BETA_OFF = A_OFF + GDN_HEADS
SCB_OFF = BETA_OFF + GDN_HEADS
SCC_OFF = SCB_OFF + SC_WIDTH
SCX_OFF = SCC_OFF + SC_WIDTH
GA_OFF = SCX_OFF + SC_WIDTH
GB_OFF = GA_OFF + D_MODEL
IN_COLS = GB_OFF + D_MODEL

kernel_name = "cond_hybrid_gdn_shortconv_block"


def rms_norm(x, w):
    xf = x.astype(jnp.float32)
    y = xf * lax.rsqrt(jnp.mean(xf * xf, axis=-1, keepdims=True) + EPS)
    return (y * w.astype(jnp.float32)).astype(x.dtype)


def l2_normalize(x):
    xf = x.astype(jnp.float32)
    return (xf * lax.rsqrt(jnp.sum(xf * xf, axis=-1, keepdims=True) + EPS)).astype(x.dtype)


def modulate(h, shift, scale):
    return h * (1.0 + scale[:, None, :]) + shift[:, None, :]


def causal_depthwise_conv(x, w):
    width, ch = w.shape
    return lax.conv_general_dilated(
        x, w[:, None, :].astype(x.dtype), window_strides=(1,), padding=[(width - 1, 0)],
        dimension_numbers=("NWC", "WIO", "NWC"), feature_group_count=ch)


def gated_delta_rule_chunked(q, k, v, g, beta):
    f32 = jnp.float32
    out_dtype = v.dtype
    b, s, h, dk = q.shape
    dv = v.shape[-1]
    n = s // CHUNK

    def to_chunks(t):
        t = t.astype(f32).reshape((b, n, CHUNK, h) + t.shape[3:])
        return jnp.moveaxis(t, 3, 1)

    q, k, v, g, beta = map(to_chunks, (q, k, v, g, beta))
    q = q * (dk ** -0.5)
    g = jnp.cumsum(g, axis=-1)
    idx = jnp.arange(CHUNK)
    causal = idx[:, None] >= idx[None, :]
    strict = idx[:, None] > idx[None, :]
    decay = jnp.exp(jnp.where(causal, g[..., :, None] - g[..., None, :], -jnp.inf))

    k_beta = k * beta[..., None]
    v_beta = v * beta[..., None]
    lower = jnp.where(strict, jnp.einsum("bhncd,bhnsd->bhncs", k_beta, k) * decay, 0.0)
    eye = jnp.eye(CHUNK, dtype=f32)
    t_inv = lax.linalg.triangular_solve(eye + lower, jnp.broadcast_to(eye, lower.shape),
                                        left_side=True, lower=True, unit_diagonal=True)
    eg = jnp.exp(g)
    u = jnp.einsum("bhncs,bhnse->bhnce", t_inv, v_beta)
    w = jnp.einsum("bhncs,bhnsd->bhncd", t_inv, k_beta * eg[..., None])
    qk = jnp.where(causal, jnp.einsum("bhncd,bhnsd->bhncs", q, k) * decay, 0.0)
    q_g = q * eg[..., None]
    g_last = g[..., -1]
    k_dec = k * jnp.exp(g_last[..., None] - g)[..., None]

    def step(state, xs):
        q_c, qk_c, u_c, w_c, k_c, gl = xs
        v_new = u_c - jnp.einsum("bhcd,bhde->bhce", w_c, state)
        o = jnp.einsum("bhcd,bhde->bhce", q_c, state) + jnp.einsum("bhcs,bhse->bhce", qk_c, v_new)
        state = state * jnp.exp(gl)[..., None, None] + jnp.einsum("bhcd,bhce->bhde", k_c, v_new)
        return state, o

    xs = tuple(jnp.moveaxis(t, 2, 0) for t in (q_g, qk, u, w, k_dec, g_last))
    state0 = jnp.zeros((b, h, dk, dv), f32)
    _, o = lax.scan(step, state0, xs)
    o = jnp.transpose(o, (1, 0, 3, 2, 4)).reshape(b, s, h, dv)
    return o.astype(out_dtype)


def hybrid_mixer(h, w_in, gdn_conv_w, a_log, dt_bias, gdn_norm_w, w_gdn_proj, sc_conv_w, w_sc_out, w_o):
    b, s, _ = h.shape
    p = h @ w_in
    qkv = jax.nn.silu(causal_depthwise_conv(p[..., Q_OFF:Z_OFF], gdn_conv_w))
    q = l2_normalize(qkv[..., Q_OFF:K_OFF].reshape(b, s, GDN_HEADS, GDN_DK))
    k = l2_normalize(qkv[..., K_OFF:V_OFF].reshape(b, s, GDN_HEADS, GDN_DK))
    v = qkv[..., V_OFF:Z_OFF].reshape(b, s, GDN_HEADS, GDN_DV)
    z = p[..., Z_OFF:A_OFF].reshape(b, s, GDN_HEADS, GDN_DV)
    g = -jnp.exp(a_log) * jax.nn.softplus(p[..., A_OFF:BETA_OFF] + dt_bias)
    beta = jax.nn.sigmoid(p[..., BETA_OFF:SCB_OFF])
    o = gated_delta_rule_chunked(q, k, v, g, beta)
    o = rms_norm(o, gdn_norm_w) * jax.nn.silu(z)
    y_a = o.reshape(b, s, V_W) @ w_gdn_proj
    gb = p[..., SCB_OFF:SCC_OFF]
    gc = p[..., SCC_OFF:SCX_OFF]
    xin = p[..., SCX_OFF:GA_OFF]
    y_b = (gb * causal_depthwise_conv(gc * xin, sc_conv_w)) @ w_sc_out
    gate_a = jax.nn.sigmoid(p[..., GA_OFF:GB_OFF])
    gate_b = jax.nn.sigmoid(p[..., GB_OFF:IN_COLS])
    return (gate_a * y_a + gate_b * y_b) @ w_o


def swiglu(h, w_ffn_in, w_ffn_out):
    gu = h @ w_ffn_in
    return (jax.nn.silu(gu[..., :D_FF]) * gu[..., D_FF:]) @ w_ffn_out


def setup_inputs(seed: int = 0) -> dict:
    key = jax.random.key(seed)
    ks = jax.random.split(key, 22)
    nrm = jax.random.normal
    D = D_MODEL
    x = nrm(ks[0], (BATCH, SEQ, D), jnp.float32)
    c = nrm(ks[1], (BATCH, D), jnp.float32)
    w_ada = nrm(ks[2], (DEPTH, D, N_MOD * D), jnp.float32) * (0.5 * D ** -0.5)
    b_ada = 0.02 * nrm(ks[3], (DEPTH, N_MOD * D), jnp.float32)
    norm1_w = 1.0 + 0.02 * nrm(ks[4], (DEPTH, D), jnp.float32)
    w_in = nrm(ks[5], (DEPTH, D, IN_COLS), jnp.float32) * D ** -0.5
    gdn_conv_w = nrm(ks[6], (DEPTH, GDN_CONV, QKV_W), jnp.float32) * GDN_CONV ** -0.5
    gdn_a_log = jnp.log(jax.random.uniform(ks[7], (DEPTH, GDN_HEADS), jnp.float32, 1.0, 16.0))
    dt = jnp.exp(jax.random.uniform(ks[8], (DEPTH, GDN_HEADS), jnp.float32,
                                    float(np.log(1e-3)), float(np.log(1e-1))))
    gdn_dt_bias = jnp.log(jnp.expm1(dt))
    gdn_norm_w = 1.0 + 0.02 * nrm(ks[9], (DEPTH, GDN_DV), jnp.float32)
    w_gdn_proj = nrm(ks[10], (DEPTH, V_W, D), jnp.float32) * V_W ** -0.5
    sc_conv_w = nrm(ks[11], (DEPTH, SC_CONV, SC_WIDTH), jnp.float32) * SC_CONV ** -0.5
    w_sc_out = nrm(ks[12], (DEPTH, SC_WIDTH, D), jnp.float32) * SC_WIDTH ** -0.5
    w_o = nrm(ks[13], (DEPTH, D, D), jnp.float32) * D ** -0.5
    norm2_w = 1.0 + 0.02 * nrm(ks[14], (DEPTH, D), jnp.float32)
    w_ffn_in = nrm(ks[15], (DEPTH, D, 2 * D_FF), jnp.float32) * D ** -0.5
    w_ffn_out = nrm(ks[16], (DEPTH, D_FF, D), jnp.float32) * D_FF ** -0.5
    w_ada_f = nrm(ks[17], (D, 2 * D), jnp.float32) * (0.5 * D ** -0.5)
    b_ada_f = 0.02 * nrm(ks[18], (2 * D,), jnp.float32)
    normf_w = 1.0 + 0.02 * nrm(ks[19], (D,), jnp.float32)
    return {"x": x, "c": c, "w_ada": w_ada, "b_ada": b_ada, "norm1_w": norm1_w, "w_in": w_in,
            "gdn_conv_w": gdn_conv_w, "gdn_a_log": gdn_a_log, "gdn_dt_bias": gdn_dt_bias,
            "gdn_norm_w": gdn_norm_w, "w_gdn_proj": w_gdn_proj, "sc_conv_w": sc_conv_w,
            "w_sc_out": w_sc_out, "w_o": w_o, "norm2_w": norm2_w, "w_ffn_in": w_ffn_in,
            "w_ffn_out": w_ffn_out, "w_ada_f": w_ada_f, "b_ada_f": b_ada_f, "normf_w": normf_w}


def reference(x, c, w_ada, b_ada, norm1_w, w_in, gdn_conv_w, gdn_a_log, gdn_dt_bias, gdn_norm_w,
              w_gdn_proj, sc_conv_w, w_sc_out, w_o, norm2_w, w_ffn_in, w_ffn_out, w_ada_f, b_ada_f,
              normf_w):
    c_act = jax.nn.silu(c)
    for l in range(DEPTH):
        mod = c_act @ w_ada[l] + b_ada[l]
        sh1, sc1, g1, sh2, sc2, g2 = jnp.split(mod, N_MOD, axis=-1)
        h = modulate(rms_norm(x, norm1_w[l]), sh1, sc1)
        mix = hybrid_mixer(h, w_in[l], gdn_conv_w[l], gdn_a_log[l], gdn_dt_bias[l], gdn_norm_w[l],
                           w_gdn_proj[l], sc_conv_w[l], w_sc_out[l], w_o[l])
        x = x + g1[:, None, :] * mix
        h = modulate(rms_norm(x, norm2_w[l]), sh2, sc2)
        x = x + g2[:, None, :] * swiglu(h, w_ffn_in[l], w_ffn_out[l])
    shf, scf = jnp.split(c_act @ w_ada_f + b_ada_f, 2, axis=-1)
    return modulate(rms_norm(x, normf_w), shf, scf)
```

```python
import functools

import jax
import jax.numpy as jnp
from jax import lax
from jax.experimental import pallas as pl
from jax.experimental.pallas import tpu as pltpu

D_MODEL = 1024
GDN_HEADS = 8
HEAD_DIM = 128
GDN_CONV = 4
SC_CONV = 3
CHUNK = 64
EPS = 1e-6
N_MOD = 6
D_FF = 2816
AB_PAD = 128

C_Q, C_K, C_V, C_Z, C_SCB, C_SCC, C_SCX, C_GA, C_GB, C_AB = (
    0, 1024, 2048, 3072, 4096, 5120, 6144, 7168, 8192, 9216)
W_ALL_COLS = C_AB + AB_PAD

TM_IN = 256
TM_GDN = 256
TM_FFN = 512
FF_SLAB = 256
CARRY = 8

BF16 = jnp.bfloat16
F32 = jnp.float32
MIB = 1024 * 1024


def _mm(a, b):
    return jnp.dot(a.astype(BF16), b.astype(BF16), preferred_element_type=F32)


def _mm_nt(a, b):
    return lax.dot_general(a.astype(BF16), b.astype(BF16), (((1,), (1,)), ((), ())),
                           preferred_element_type=F32)


def _mm_tn(a, b):
    return lax.dot_general(a.astype(BF16), b.astype(BF16), (((0,), (0,)), ((), ())),
                           preferred_element_type=F32)


def _silu(x):
    return x * jax.nn.sigmoid(x)


def _rms(x):
    return x * lax.rsqrt(jnp.mean(x * x, axis=-1, keepdims=True) + EPS)


def _mod_kernel(c_ref, w_ref, b_ref, o_ref):
    o_ref[...] = _mm(_silu(c_ref[...]), w_ref[...]) + b_ref[...]


def _mod_call(c, w, b):
    bsz, d = c.shape
    n = w.shape[1]
    tn = 1024
    return pl.pallas_call(
        _mod_kernel,
        grid=(n // tn,),
        in_specs=[pl.BlockSpec((bsz, d), lambda j: (0, 0)),
                  pl.BlockSpec((d, tn), lambda j: (0, j)),
                  pl.BlockSpec((1, tn), lambda j: (0, j))],
        out_specs=pl.BlockSpec((bsz, tn), lambda j: (0, j)),
        out_shape=jax.ShapeDtypeStruct((bsz, n), F32),
        compiler_params=pltpu.CompilerParams(dimension_semantics=("arbitrary",),
                                             vmem_limit_bytes=24 * MIB),
        name="mod_rows",
    )(c, w, b.reshape(1, n))


def _in_kernel(x_ref, mod_ref, n1_ref, w_ref, cw_ref, scw_ref, alog_ref, dtb_ref, wsc_ref, tri_ref,
               q_ref, k_ref, v_ref, zs_ref, ga_ref, ybg_ref, gbc_ref, gbr_ref, pbuf):
    t = pl.program_id(1)
    tm = TM_IN

    @pl.when(t == 0)
    def _():
        pbuf[:, 0:CARRY, :] = jnp.zeros((4, CARRY, D_MODEL), F32)

    x = x_ref[0]
    sh1 = mod_ref[0, 0:1, :]
    sc1 = mod_ref[0, 1:2, :]
    h = (_rms(x) * n1_ref[...] * (1.0 + sc1) + sh1).astype(BF16)

    def proj(c0, width=D_MODEL):
        return jnp.dot(h, w_ref[:, c0:c0 + width], preferred_element_type=F32)

    def causal_conv(slot, p, taps, width):
        pbuf[slot, CARRY:CARRY + tm, :] = p
        acc = taps[width - 1:width, :] * p
        for j in range(width - 1):
            sh = width - 1 - j
            acc = acc + taps[j:j + 1, :] * pbuf[slot, CARRY - sh:CARRY - sh + tm, :]
        pbuf[slot, 0:CARRY, :] = pbuf[slot, tm:tm + CARRY, :]
        return acc

    for slot, (c0, out_ref, norm, scale) in enumerate((
            (C_Q, q_ref, True, HEAD_DIM ** -0.5), (C_K, k_ref, True, 1.0), (C_V, v_ref, False, 1.0))):
        taps = cw_ref[:, c0:c0 + D_MODEL]
        y = _silu(causal_conv(slot, proj(c0), taps, GDN_CONV))
        for hd in range(GDN_HEADS):
            yh = y[:, hd * HEAD_DIM:(hd + 1) * HEAD_DIM]
            if norm:
                yh = yh * lax.rsqrt(jnp.sum(yh * yh, axis=-1, keepdims=True) + EPS)
                if scale != 1.0:
                    yh = yh * scale
            out_ref[0, hd] = yh.astype(out_ref.dtype)

    zs = _silu(proj(C_Z))
    for hd in range(GDN_HEADS):
        zs_ref[0, hd] = zs[:, hd * HEAD_DIM:(hd + 1) * HEAD_DIM].astype(zs_ref.dtype)

    pab = proj(C_AB, AB_PAD)
    lane = lax.broadcasted_iota(jnp.int32, (tm, AB_PAD), 1)
    xg = pab + dtb_ref[...]
    softplus = jnp.maximum(xg, 0.0) + jnp.log1p(jnp.exp(-jnp.abs(xg)))
    g = -jnp.exp(alog_ref[...]) * softplus
    gb = jnp.where(lane < GDN_HEADS, g, jax.nn.sigmoid(pab))
    g1 = gb.astype(BF16)
    r1 = gb - g1.astype(F32)
    g2 = r1.astype(BF16)
    g3 = (r1 - g2.astype(F32)).astype(BF16)
    tri = tri_ref[...]
    cum = (jnp.dot(tri, g1, preferred_element_type=F32) + jnp.dot(tri, g2, preferred_element_type=F32)
           + jnp.dot(tri, g3, preferred_element_type=F32))
    gbc = jnp.where(lane < GDN_HEADS, cum, gb)
    gbc_ref[0] = gbc
    gbr_ref[0] = jnp.transpose(gbc)[0:2 * GDN_HEADS, :]

    prod = proj(C_SCC) * proj(C_SCX)
    u = proj(C_SCB) * causal_conv(3, prod, scw_ref[...], SC_CONV)
    yb = _mm(u, wsc_ref[...])
    ybg_ref[0] = (jax.nn.sigmoid(proj(C_GB)) * yb).astype(ybg_ref.dtype)
    ga_ref[0] = jax.nn.sigmoid(proj(C_GA)).astype(ga_ref.dtype)


def _in_call(x, mod3, norm1_w, w_all, conv_w, sc_conv_w, alog_row, dtb_row, w_sc_out, tri):
    bsz, seq, d = x.shape
    tm = TM_IN
    nt = seq // tm
    const2 = lambda b, t: (0, 0)
    resident = dict(pipeline_mode=pl.Buffered(1))
    head_spec = pl.BlockSpec((1, GDN_HEADS, tm, HEAD_DIM), lambda b, t: (b, 0, t, 0))
    tok_spec = pl.BlockSpec((1, tm, d), lambda b, t: (b, t, 0))
    head_shape = jax.ShapeDtypeStruct((bsz, GDN_HEADS, seq, HEAD_DIM), BF16)
    return pl.pallas_call(
        _in_kernel,
        grid=(bsz, nt),
        in_specs=[tok_spec,
                  pl.BlockSpec((1, N_MOD, d), lambda b, t: (b, 0, 0)),
                  pl.BlockSpec((1, d), const2),
                  pl.BlockSpec((d, W_ALL_COLS), const2, **resident),
                  pl.BlockSpec((GDN_CONV, 3 * d), const2),
                  pl.BlockSpec((SC_CONV, d), const2),
                  pl.BlockSpec((1, AB_PAD), const2),
                  pl.BlockSpec((1, AB_PAD), const2),
                  pl.BlockSpec((d, d), const2, **resident),
                  pl.BlockSpec((tm, tm), const2)],
        out_specs=[head_spec, head_spec, head_spec, head_spec, tok_spec, tok_spec,
                   pl.BlockSpec((1, tm, AB_PAD), lambda b, t: (b, t, 0)),
                   pl.BlockSpec((1, 2 * GDN_HEADS, tm), lambda b, t: (b, 0, t))],
        out_shape=[head_shape, head_shape, head_shape, head_shape,
                   jax.ShapeDtypeStruct((bsz, seq, d), BF16),
                   jax.ShapeDtypeStruct((bsz, seq, d), BF16),
                   jax.ShapeDtypeStruct((bsz, seq, AB_PAD), F32),
                   jax.ShapeDtypeStruct((bsz, 2 * GDN_HEADS, seq), F32)],
        scratch_shapes=[pltpu.VMEM((4, CARRY + tm, d), F32)],
        compiler_params=pltpu.CompilerParams(dimension_semantics=("arbitrary", "arbitrary"),
                                             vmem_limit_bytes=52 * MIB),
        name="in_proj",
    )(x, mod3, norm1_w, w_all, conv_w, sc_conv_w, alog_row, dtb_row, w_sc_out, tri)


def _tri_inverse(low, blk16, blk32, eye):
    a1 = jnp.where(blk16, -low, 0.0)
    a2 = _mm(a1, a1)
    a4 = _mm(a2, a2)
    a8 = _mm(a4, a4)
    p = eye + a1
    p = p + _mm(p, a2)
    p = p + _mm(p, a4)
    p = p + _mm(p, a8)
    o1 = jnp.where(jnp.logical_and(blk32, jnp.logical_not(blk16)), low, 0.0)
    p = p - _mm(_mm(p, o1), p)
    o2 = jnp.where(blk32, 0.0, low)
    p = p - _mm(_mm(p, o2), p)
    return p


def _gdn_kernel(q_ref, k_ref, v_ref, zs_ref, gbc_ref, gbr_ref, ga_ref, ybg_ref, x_ref, mod_ref,
                nw_ref, wg_ref, wo_ref, out_ref, state_ref, bc_ref, acc_ref):
    t = pl.program_id(1)
    tm = TM_GDN
    nchunk = tm // CHUNK

    @pl.when(t == 0)
    def _():
        state_ref[...] = jnp.zeros_like(state_ref)

    gbc = gbc_ref[0]
    for j in range(2 * GDN_HEADS):
        bc_ref[j] = jnp.broadcast_to(gbc[:, j:j + 1], (tm, HEAD_DIM))

    ri = lax.broadcasted_iota(jnp.int32, (CHUNK, CHUNK), 0)
    ci = lax.broadcasted_iota(jnp.int32, (CHUNK, CHUNK), 1)
    causal = ri >= ci
    strict = ri > ci
    blk16 = (ri // 16) == (ci // 16)
    blk32 = (ri // 32) == (ci // 32)
    eye = jnp.where(ri == ci, 1.0, 0.0).astype(F32)
    nw = nw_ref[...]

    acc_ref[...] = jnp.zeros_like(acc_ref)

    def one_head(hd):
        qh = q_ref[0, hd].astype(F32)
        kh = k_ref[0, hd].astype(F32)
        vh = v_ref[0, hd].astype(F32)
        gcb = bc_ref[hd]
        bb = bc_ref[GDN_HEADS + hd]
        grow = gbr_ref[0, pl.ds(hd, 1), :]
        a = jnp.exp(gcb)
        kb = kh * bb
        vb = vh * bb
        kba = kb * a
        qa = qh * a
        s = state_ref[hd]
        outs = []
        for c in range(nchunk):
            r0 = c * CHUNK
            k_c = kh[r0:r0 + CHUNK]
            diff = gcb[r0:r0 + CHUNK, 0:CHUNK] - grow[:, r0:r0 + CHUNK]
            dec = jnp.where(causal, jnp.exp(jnp.minimum(diff, 0.0)), 0.0)
            low = jnp.where(strict, _mm_nt(kb[r0:r0 + CHUNK], k_c) * dec, 0.0)
            tinv = _tri_inverse(low, blk16, blk32, eye)
            uw = _mm(tinv, jnp.concatenate([vb[r0:r0 + CHUNK], kba[r0:r0 + CHUNK]], axis=1))
            u = uw[:, 0:HEAD_DIM]
            w = uw[:, HEAD_DIM:2 * HEAD_DIM]
            qk = jnp.where(causal, _mm_nt(qh[r0:r0 + CHUNK], k_c) * dec, 0.0)
            ws_qs = _mm(jnp.concatenate([w, qa[r0:r0 + CHUNK]], axis=0), s)
            v_new = u - ws_qs[0:CHUNK]
            outs.append(ws_qs[CHUNK:2 * CHUNK] + _mm(qk, v_new))
            gl = gcb[r0 + CHUNK - 1:r0 + CHUNK, :]
            k_dec = k_c * jnp.exp(gl - gcb[r0:r0 + CHUNK])
            s = s * jnp.exp(gl) + _mm_tn(k_dec, v_new)
        state_ref[hd] = s
        o = jnp.concatenate(outs, axis=0)
        return (_rms(o) * nw * zs_ref[0, hd].astype(F32)).astype(BF16)

    def pair_body(p, carry):
        o2 = jnp.concatenate([one_head(2 * p), one_head(2 * p + 1)], axis=1)
        acc_ref[...] += jnp.dot(o2, wg_ref[p], preferred_element_type=F32)
        return carry

    lax.fori_loop(0, GDN_HEADS // 2, pair_body, 0)

    merged = ga_ref[0].astype(F32) * acc_ref[...] + ybg_ref[0].astype(F32)
    mix = _mm(merged, wo_ref[...])
    out_ref[0] = x_ref[0] + mod_ref[0, 2:3, :] * mix


def _gdn_call(q, k, v, zs, gbc, gbr, ga, ybg, x, mod3, gdn_norm_w, wg, wo):
    bsz, seq, d = x.shape
    tm = TM_GDN
    nt = seq // tm
    const2 = lambda b, t: (0, 0)
    resident = dict(pipeline_mode=pl.Buffered(1))
    head_spec = pl.BlockSpec((1, GDN_HEADS, tm, HEAD_DIM), lambda b, t: (b, 0, t, 0))
    tok_spec = pl.BlockSpec((1, tm, d), lambda b, t: (b, t, 0))
    return pl.pallas_call(
        _gdn_kernel,
        grid=(bsz, nt),
        in_specs=[head_spec, head_spec, head_spec, head_spec,
                  pl.BlockSpec((1, tm, AB_PAD), lambda b, t: (b, t, 0)),
                  pl.BlockSpec((1, 2 * GDN_HEADS, tm), lambda b, t: (b, 0, t)),
                  tok_spec, tok_spec, tok_spec,
                  pl.BlockSpec((1, N_MOD, d), lambda b, t: (b, 0, 0)),
                  pl.BlockSpec((1, HEAD_DIM), const2),
                  pl.BlockSpec((GDN_HEADS // 2, 2 * HEAD_DIM, d), lambda b, t: (0, 0, 0), **resident),
                  pl.BlockSpec((d, d), const2, **resident)],
        out_specs=tok_spec,
        out_shape=jax.ShapeDtypeStruct((bsz, seq, d), F32),
        scratch_shapes=[pltpu.VMEM((GDN_HEADS, HEAD_DIM, HEAD_DIM), F32),
                        pltpu.VMEM((2 * GDN_HEADS, tm, HEAD_DIM), F32),
                        pltpu.VMEM((tm, d), F32)],
        compiler_params=pltpu.CompilerParams(dimension_semantics=("arbitrary", "arbitrary"),
                                             vmem_limit_bytes=40 * MIB),
        name="gdn_mix",
    )(q, k, v, zs, gbc, gbr, ga, ybg, x, mod3, gdn_norm_w, wg, wo)


def _ffn_kernel(x_ref, mod_ref, modf_ref, n2_ref, nf_ref, win_ref, wout_ref, out_ref):
    x = x_ref[0]
    sh2 = mod_ref[0, 3:4, :]
    sc2 = mod_ref[0, 4:5, :]
    g2 = mod_ref[0, 5:6, :]
    h = (_rms(x) * n2_ref[...] * (1.0 + sc2) + sh2).astype(BF16)
    acc = jnp.zeros(x.shape, F32)
    for s0 in range(0, D_FF, FF_SLAB):
        gate = jnp.dot(h, win_ref[:, s0:s0 + FF_SLAB], preferred_element_type=F32)
        up = jnp.dot(h, win_ref[:, D_FF + s0:D_FF + s0 + FF_SLAB], preferred_element_type=F32)
        act = (_silu(gate) * up).astype(BF16)
        acc = acc + jnp.dot(act, wout_ref[s0:s0 + FF_SLAB, :], preferred_element_type=F32)
    x2 = x + g2 * acc
    out_ref[0] = _rms(x2) * nf_ref[...] * (1.0 + modf_ref[0, 1:2, :]) + modf_ref[0, 0:1, :]


def _ffn_call(x, mod3, modf3, norm2_w, normf_w, w_in, w_out):
    bsz, seq, d = x.shape
    tm = TM_FFN
    const2 = lambda b, t: (0, 0)
    resident = dict(pipeline_mode=pl.Buffered(1))
    tok_spec = pl.BlockSpec((1, tm, d), lambda b, t: (b, t, 0))
    return pl.pallas_call(
        _ffn_kernel,
        grid=(bsz, seq // tm),
        in_specs=[tok_spec,
                  pl.BlockSpec((1, N_MOD, d), lambda b, t: (b, 0, 0)),
                  pl.BlockSpec((1, 2, d), lambda b, t: (b, 0, 0)),
                  pl.BlockSpec((1, d), const2),
                  pl.BlockSpec((1, d), const2),
                  pl.BlockSpec((d, 2 * D_FF), const2, **resident),
                  pl.BlockSpec((D_FF, d), const2, **resident)],
        out_specs=tok_spec,
        out_shape=jax.ShapeDtypeStruct((bsz, seq, d), F32),
        compiler_params=pltpu.CompilerParams(dimension_semantics=("arbitrary", "arbitrary"),
                                             vmem_limit_bytes=48 * MIB),
        name="ffn_final",
    )(x, mod3, modf3, norm2_w, normf_w, w_in, w_out)


def kernel(x, c, w_ada, b_ada, norm1_w, w_in, gdn_conv_w, gdn_a_log, gdn_dt_bias, gdn_norm_w, w_gdn_proj,
           sc_conv_w, w_sc_out, w_o, norm2_w, w_ffn_in, w_ffn_out, w_ada_f, b_ada_f, normf_w):
    bsz, seq, d = x.shape
    assert w_ada.shape[0] == 1 and d == D_MODEL and seq % TM_FFN == 0
    ab0 = C_Z + D_MODEL
    nab = 2 * GDN_HEADS

    mod3 = _mod_call(c, w_ada[0], b_ada[0]).reshape(bsz, N_MOD, d)
    modf3 = _mod_call(c, w_ada_f, b_ada_f).reshape(bsz, 2, d)

    w_l = w_in[0]
    w_all = jnp.concatenate(
        [w_l[:, :ab0], w_l[:, ab0 + nab:], jnp.pad(w_l[:, ab0:ab0 + nab], ((0, 0), (0, AB_PAD - nab)))],
        axis=1).astype(BF16)
    pad_row = lambda v: jnp.pad(v, (0, AB_PAD - GDN_HEADS)).reshape(1, AB_PAD)
    ii = jnp.arange(TM_IN)
    tri = ((ii[:, None] >= ii[None, :]) & ((ii[:, None] // CHUNK) == (ii[None, :] // CHUNK))).astype(BF16)

    q, k, v, zs, ga, ybg, gbc, gbr = _in_call(
        x, mod3, norm1_w, w_all, gdn_conv_w[0], sc_conv_w[0], pad_row(gdn_a_log[0]), pad_row(gdn_dt_bias[0]),
        w_sc_out[0].astype(BF16), tri)

    wg = w_gdn_proj[0].astype(BF16).reshape(GDN_HEADS // 2, 2 * HEAD_DIM, d)
    x1 = _gdn_call(q, k, v, zs, gbc, gbr, ga, ybg, x, mod3, gdn_norm_w, wg, w_o[0].astype(BF16))

    return _ffn_call(x1, mod3, modf3, norm2_w, normf_w.reshape(1, d), w_ffn_in[0].astype(BF16), w_ffn_out[0].astype(BF16))
```

```python
import functools

import jax
import jax.numpy as jnp
from jax import lax
from jax.experimental import pallas as pl
from jax.experimental.pallas import tpu as pltpu

D_MODEL = 1024
GDN_HEADS = 8
HEAD_DIM = 128
GDN_CONV = 4
SC_CONV = 3
CHUNK = 64
EPS = 1e-6
N_MOD = 6
D_FF = 2816
AB_PAD = 128

C_Q, C_K, C_V, C_Z, C_SCB, C_SCC, C_SCX, C_GA, C_GB, C_AB = (
    0, 1024, 2048, 3072, 4096, 5120, 6144, 7168, 8192, 9216)
W_ALL_COLS = C_AB + AB_PAD

TM_IN = 256
TM_GDN = 256
TM_FFN = 512
FF_SLAB = 256
CARRY = 8

BF16 = jnp.bfloat16
F32 = jnp.float32
MIB = 1024 * 1024


def _mm(a, b):
    return jnp.dot(a.astype(BF16), b.astype(BF16), preferred_element_type=F32)


def _mm_nt(a, b):
    return lax.dot_general(a.astype(BF16), b.astype(BF16), (((1,), (1,)), ((), ())),
                           preferred_element_type=F32)


def _mm_tn(a, b):
    return lax.dot_general(a.astype(BF16), b.astype(BF16), (((0,), (0,)), ((), ())),
                           preferred_element_type=F32)


def _bmm(a, b):
    return lax.dot_general(a.astype(BF16), b.astype(BF16), (((2,), (1,)), ((0,), (0,))),
                           preferred_element_type=F32)


def _bmm_nt(a, b):
    return lax.dot_general(a.astype(BF16), b.astype(BF16), (((2,), (2,)), ((0,), (0,))),
                           preferred_element_type=F32)


def _bmm_tn(a, b):
    return lax.dot_general(a.astype(BF16), b.astype(BF16), (((1,), (1,)), ((0,), (0,))),
                           preferred_element_type=F32)


def _silu(x):
    return x * jax.nn.sigmoid(x)


def _rms(x):
    return x * lax.rsqrt(jnp.mean(x * x, axis=-1, keepdims=True) + EPS)


def _mod_kernel(c_ref, w_ref, b_ref, o_ref):
    o_ref[...] = _mm(_silu(c_ref[...]), w_ref[...]) + b_ref[...]


def _mod_call(c, w, b):
    bsz, d = c.shape
    n = w.shape[1]
    tn = 1024
    return pl.pallas_call(
        _mod_kernel,
        grid=(n // tn,),
        in_specs=[pl.BlockSpec((bsz, d), lambda j: (0, 0)),
                  pl.BlockSpec((d, tn), lambda j: (0, j)),
                  pl.BlockSpec((1, tn), lambda j: (0, j))],
        out_specs=pl.BlockSpec((bsz, tn), lambda j: (0, j)),
        out_shape=jax.ShapeDtypeStruct((bsz, n), F32),
        compiler_params=pltpu.CompilerParams(dimension_semantics=("arbitrary",),
                                             vmem_limit_bytes=24 * MIB),
        name="mod_rows",
    )(c, w, b.reshape(1, n))


def _in_kernel(x_ref, mod_ref, n1_ref, w_ref, cw_ref, scw_ref, alog_ref, dtb_ref, wsc_ref, tri_ref,
               q_ref, k_ref, v_ref, zs_ref, ga_ref, ybg_ref, gbc_ref, gbr_ref, pbuf):
    t = pl.program_id(1)
    tm = TM_IN

    @pl.when(t == 0)
    def _():
        pbuf[:, 0:CARRY, :] = jnp.zeros((4, CARRY, D_MODEL), F32)

    x = x_ref[0]
    sh1 = mod_ref[0, 0:1, :]
    sc1 = mod_ref[0, 1:2, :]
    h = (_rms(x) * n1_ref[...] * (1.0 + sc1) + sh1).astype(BF16)

    def proj(c0, width=D_MODEL):
        return jnp.dot(h, w_ref[:, c0:c0 + width], preferred_element_type=F32)

    def causal_conv(slot, p, taps, width):
        pbuf[slot, CARRY:CARRY + tm, :] = p
        acc = taps[width - 1:width, :] * p
        for j in range(width - 1):
            sh = width - 1 - j
            acc = acc + taps[j:j + 1, :] * pbuf[slot, CARRY - sh:CARRY - sh + tm, :]
        pbuf[slot, 0:CARRY, :] = pbuf[slot, tm:tm + CARRY, :]
        return acc

    for slot, (c0, out_ref, norm, scale) in enumerate((
            (C_Q, q_ref, True, HEAD_DIM ** -0.5), (C_K, k_ref, True, 1.0), (C_V, v_ref, False, 1.0))):
        taps = cw_ref[:, c0:c0 + D_MODEL]
        y = _silu(causal_conv(slot, proj(c0), taps, GDN_CONV))
        for hd in range(GDN_HEADS):
            yh = y[:, hd * HEAD_DIM:(hd + 1) * HEAD_DIM]
            if norm:
                yh = yh * lax.rsqrt(jnp.sum(yh * yh, axis=-1, keepdims=True) + EPS)
                if scale != 1.0:
                    yh = yh * scale
            out_ref[0, hd] = yh.astype(out_ref.dtype)

    zs = _silu(proj(C_Z))
    for hd in range(GDN_HEADS):
        zs_ref[0, hd] = zs[:, hd * HEAD_DIM:(hd + 1) * HEAD_DIM].astype(zs_ref.dtype)

    pab = proj(C_AB, AB_PAD)
    lane = lax.broadcasted_iota(jnp.int32, (tm, AB_PAD), 1)
    xg = pab + dtb_ref[...]
    softplus = jnp.maximum(xg, 0.0) + jnp.log1p(jnp.exp(-jnp.abs(xg)))
    g = -jnp.exp(alog_ref[...]) * softplus
    gb = jnp.where(lane < GDN_HEADS, g, jax.nn.sigmoid(pab))
    g1 = gb.astype(BF16)
    r1 = gb - g1.astype(F32)
    g2 = r1.astype(BF16)
    g3 = (r1 - g2.astype(F32)).astype(BF16)
    tri = tri_ref[...]
    cum = (jnp.dot(tri, g1, preferred_element_type=F32) + jnp.dot(tri, g2, preferred_element_type=F32)
           + jnp.dot(tri, g3, preferred_element_type=F32))
    gbc = jnp.where(lane < GDN_HEADS, cum, gb)
    gbc_ref[0] = gbc
    gbr_ref[0] = jnp.transpose(gbc)[0:2 * GDN_HEADS, :]

    prod = proj(C_SCC) * proj(C_SCX)
    u = proj(C_SCB) * causal_conv(3, prod, scw_ref[...], SC_CONV)
    yb = _mm(u, wsc_ref[...])
    ybg_ref[0] = (jax.nn.sigmoid(proj(C_GB)) * yb).astype(ybg_ref.dtype)
    ga_ref[0] = jax.nn.sigmoid(proj(C_GA)).astype(ga_ref.dtype)


def _in_call(x, mod3, norm1_w, w_all, conv_w, sc_conv_w, alog_row, dtb_row, w_sc_out, tri):
    bsz, seq, d = x.shape
    tm = TM_IN
    nt = seq // tm
    const2 = lambda b, t: (0, 0)
    resident = dict(pipeline_mode=pl.Buffered(1))
    head_spec = pl.BlockSpec((1, GDN_HEADS, tm, HEAD_DIM), lambda b, t: (b, 0, t, 0))
    tok_spec = pl.BlockSpec((1, tm, d), lambda b, t: (b, t, 0))
    head_shape = jax.ShapeDtypeStruct((bsz, GDN_HEADS, seq, HEAD_DIM), BF16)
    return pl.pallas_call(
        _in_kernel,
        grid=(bsz, nt),
        in_specs=[tok_spec,
                  pl.BlockSpec((1, N_MOD, d), lambda b, t: (b, 0, 0)),
                  pl.BlockSpec((1, d), const2),
                  pl.BlockSpec((d, W_ALL_COLS), const2, **resident),
                  pl.BlockSpec((GDN_CONV, 3 * d), const2),
                  pl.BlockSpec((SC_CONV, d), const2),
                  pl.BlockSpec((1, AB_PAD), const2),
                  pl.BlockSpec((1, AB_PAD), const2),
                  pl.BlockSpec((d, d), const2, **resident),
                  pl.BlockSpec((tm, tm), const2)],
        out_specs=[head_spec, head_spec, head_spec, head_spec, tok_spec, tok_spec,
                   pl.BlockSpec((1, tm, AB_PAD), lambda b, t: (b, t, 0)),
                   pl.BlockSpec((1, 2 * GDN_HEADS, tm), lambda b, t: (b, 0, t))],
        out_shape=[head_shape, head_shape, head_shape, head_shape,
                   jax.ShapeDtypeStruct((bsz, seq, d), BF16),
                   jax.ShapeDtypeStruct((bsz, seq, d), BF16),
                   jax.ShapeDtypeStruct((bsz, seq, AB_PAD), F32),
                   jax.ShapeDtypeStruct((bsz, 2 * GDN_HEADS, seq), F32)],
        scratch_shapes=[pltpu.VMEM((4, CARRY + tm, d), F32)],
        compiler_params=pltpu.CompilerParams(dimension_semantics=("arbitrary", "arbitrary"),
                                             vmem_limit_bytes=52 * MIB),
        name="in_proj",
    )(x, mod3, norm1_w, w_all, conv_w, sc_conv_w, alog_row, dtb_row, w_sc_out, tri)


def _tri_inverse(low, blk16, blk32, eye):
    a1 = jnp.where(blk16, -low, 0.0)
    a2 = _bmm(a1, a1)
    a4 = _bmm(a2, a2)
    a8 = _bmm(a4, a4)
    p = eye + a1
    p = p + _bmm(p, a2)
    p = p + _bmm(p, a4)
    p = p + _bmm(p, a8)
    o1 = jnp.where(jnp.logical_and(blk32, jnp.logical_not(blk16)), low, 0.0)
    p = p - _bmm(_bmm(p, o1), p)
    o2 = jnp.where(blk32, 0.0, low)
    p = p - _bmm(_bmm(p, o2), p)
    return p


def _gdn_kernel(q_ref, k_ref, v_ref, zs_ref, gbc_ref, gbr_ref, ga_ref, ybg_ref, x_ref, mod_ref,
                nw_ref, wg_ref, wo_ref, out_ref, state_ref):
    t = pl.program_id(1)
    tm = TM_GDN
    nchunk = tm // CHUNK

    @pl.when(t == 0)
    def _():
        state_ref[...] = jnp.zeros_like(state_ref)

    nh = GDN_HEADS
    nb = nh * nchunk
    gbc = gbc_ref[0]
    gbr = gbr_ref[0]

    def col_bcast(j, width):
        return jnp.broadcast_to(gbc[:, j:j + 1], (tm, width))

    gcb = jnp.stack([col_bcast(h, HEAD_DIM) for h in range(nh)]).reshape(nb, CHUNK, HEAD_DIM)
    bb = jnp.stack([col_bcast(nh + h, HEAD_DIM) for h in range(nh)]).reshape(nb, CHUNK, HEAD_DIM)
    gcol = jnp.stack([col_bcast(h, CHUNK) for h in range(nh)]).reshape(nb, CHUNK, CHUNK)
    grow = jnp.stack([jnp.broadcast_to(gbr[h:h + 1, c * CHUNK:(c + 1) * CHUNK], (CHUNK, CHUNK))
                      for h in range(nh) for c in range(nchunk)])

    ri = lax.broadcasted_iota(jnp.int32, (nb, CHUNK, CHUNK), 1)
    ci = lax.broadcasted_iota(jnp.int32, (nb, CHUNK, CHUNK), 2)
    causal = ri >= ci
    strict = ri > ci
    blk16 = (ri // 16) == (ci // 16)
    blk32 = (ri // 32) == (ci // 32)
    eye = jnp.where(ri == ci, 1.0, 0.0).astype(F32)

    q = q_ref[0].reshape(nb, CHUNK, HEAD_DIM)
    k = k_ref[0].reshape(nb, CHUNK, HEAD_DIM)
    kf = k.astype(F32)
    a = jnp.exp(gcb)
    kb = kf * bb
    vb = v_ref[0].reshape(nb, CHUNK, HEAD_DIM).astype(F32) * bb
    kba = kb * a
    qa = q.astype(F32) * a

    dec = jnp.where(causal, jnp.exp(jnp.minimum(gcol - grow, 0.0)), 0.0)
    low = jnp.where(strict, _bmm_nt(kb, k) * dec, 0.0)
    tinv = _tri_inverse(low, blk16, blk32, eye)
    uw = _bmm(tinv, jnp.concatenate([vb, kba], axis=2))
    qk = jnp.where(causal, _bmm_nt(q, k) * dec, 0.0)

    def chunk_of(arr, c):
        return arr.reshape((nh, nchunk) + arr.shape[1:])[:, c]

    s = state_ref[...]
    outs = []
    for c in range(nchunk):
        uw_c = chunk_of(uw, c)
        gcb_c = chunk_of(gcb, c)
        ws_qs = _bmm(jnp.concatenate([uw_c[:, :, HEAD_DIM:], chunk_of(qa, c)], axis=1), s)
        v_new = uw_c[:, :, 0:HEAD_DIM] - ws_qs[:, 0:CHUNK]
        outs.append(ws_qs[:, CHUNK:] + _bmm(chunk_of(qk, c), v_new))
        gl = gcb_c[:, CHUNK - 1:CHUNK, :]
        k_dec = chunk_of(kf, c) * jnp.exp(gl - gcb_c)
        s = s * jnp.exp(gl) + _bmm_tn(k_dec, v_new)
    state_ref[...] = s

    o = jnp.concatenate(outs, axis=1)
    on = (_rms(o) * nw_ref[...] * zs_ref[0].astype(F32)).astype(BF16)
    ya = jnp.dot(jnp.concatenate([on[h] for h in range(nh)], axis=1), wg_ref[...],
                 preferred_element_type=F32)

    merged = ga_ref[0].astype(F32) * ya + ybg_ref[0].astype(F32)
    mix = _mm(merged, wo_ref[...])
    out_ref[0] = x_ref[0] + mod_ref[0, 2:3, :] * mix


def _gdn_call(q, k, v, zs, gbc, gbr, ga, ybg, x, mod3, gdn_norm_w, wg, wo):
    bsz, seq, d = x.shape
    tm = TM_GDN
    nt = seq // tm
    const2 = lambda b, t: (0, 0)
    resident = dict(pipeline_mode=pl.Buffered(1))
    head_spec = pl.BlockSpec((1, GDN_HEADS, tm, HEAD_DIM), lambda b, t: (b, 0, t, 0))
    tok_spec = pl.BlockSpec((1, tm, d), lambda b, t: (b, t, 0))
    return pl.pallas_call(
        _gdn_kernel,
        grid=(bsz, nt),
        in_specs=[head_spec, head_spec, head_spec, head_spec,
                  pl.BlockSpec((1, tm, AB_PAD), lambda b, t: (b, t, 0)),
                  pl.BlockSpec((1, 2 * GDN_HEADS, tm), lambda b, t: (b, 0, t)),
                  tok_spec, tok_spec, tok_spec,
                  pl.BlockSpec((1, N_MOD, d), lambda b, t: (b, 0, 0)),
                  pl.BlockSpec((1, HEAD_DIM), const2),
                  pl.BlockSpec((d, d), const2, **resident),
                  pl.BlockSpec((d, d), const2, **resident)],
        out_specs=tok_spec,
        out_shape=jax.ShapeDtypeStruct((bsz, seq, d), F32),
        scratch_shapes=[pltpu.VMEM((GDN_HEADS, HEAD_DIM, HEAD_DIM), F32)],
        compiler_params=pltpu.CompilerParams(dimension_semantics=("arbitrary", "arbitrary"),
                                             vmem_limit_bytes=40 * MIB),
        name="gdn_mix",
    )(q, k, v, zs, gbc, gbr, ga, ybg, x, mod3, gdn_norm_w, wg, wo)


def _ffn_kernel(x_ref, mod_ref, modf_ref, n2_ref, nf_ref, win_ref, wout_ref, out_ref):
    x = x_ref[0]
    sh2 = mod_ref[0, 3:4, :]
    sc2 = mod_ref[0, 4:5, :]
    g2 = mod_ref[0, 5:6, :]
    h = (_rms(x) * n2_ref[...] * (1.0 + sc2) + sh2).astype(BF16)
    acc = jnp.zeros(x.shape, F32)
    for s0 in range(0, D_FF, FF_SLAB):
        gate = jnp.dot(h, win_ref[:, s0:s0 + FF_SLAB], preferred_element_type=F32)
        up = jnp.dot(h, win_ref[:, D_FF + s0:D_FF + s0 + FF_SLAB], preferred_element_type=F32)
        act = (_silu(gate) * up).astype(BF16)
        acc = acc + jnp.dot(act, wout_ref[s0:s0 + FF_SLAB, :], preferred_element_type=F32)
    x2 = x + g2 * acc
    out_ref[0] = _rms(x2) * nf_ref[...] * (1.0 + modf_ref[0, 1:2, :]) + modf_ref[0, 0:1, :]


def _ffn_call(x, mod3, modf3, norm2_w, normf_w, w_in, w_out):
    bsz, seq, d = x.shape
    tm = TM_FFN
    const2 = lambda b, t: (0, 0)
    resident = dict(pipeline_mode=pl.Buffered(1))
    tok_spec = pl.BlockSpec((1, tm, d), lambda b, t: (b, t, 0))
    return pl.pallas_call(
        _ffn_kernel,
        grid=(bsz, seq // tm),
        in_specs=[tok_spec,
                  pl.BlockSpec((1, N_MOD, d), lambda b, t: (b, 0, 0)),
                  pl.BlockSpec((1, 2, d), lambda b, t: (b, 0, 0)),
                  pl.BlockSpec((1, d), const2),
                  pl.BlockSpec((1, d), const2),
                  pl.BlockSpec((d, 2 * D_FF), const2, **resident),
                  pl.BlockSpec((D_FF, d), const2, **resident)],
        out_specs=tok_spec,
        out_shape=jax.ShapeDtypeStruct((bsz, seq, d), F32),
        compiler_params=pltpu.CompilerParams(dimension_semantics=("arbitrary", "arbitrary"),
                                             vmem_limit_bytes=48 * MIB),
        name="ffn_final",
    )(x, mod3, modf3, norm2_w, normf_w, w_in, w_out)


def kernel(x, c, w_ada, b_ada, norm1_w, w_in, gdn_conv_w, gdn_a_log, gdn_dt_bias, gdn_norm_w, w_gdn_proj,
           sc_conv_w, w_sc_out, w_o, norm2_w, w_ffn_in, w_ffn_out, w_ada_f, b_ada_f, normf_w):
    bsz, seq, d = x.shape
    assert w_ada.shape[0] == 1 and d == D_MODEL and seq % TM_FFN == 0
    ab0 = C_Z + D_MODEL
    nab = 2 * GDN_HEADS

    mod3 = _mod_call(c, w_ada[0], b_ada[0]).reshape(bsz, N_MOD, d)
    modf3 = _mod_call(c, w_ada_f, b_ada_f).reshape(bsz, 2, d)

    w_l = w_in[0]
    w_all = jnp.concatenate(
        [w_l[:, :ab0], w_l[:, ab0 + nab:], jnp.pad(w_l[:, ab0:ab0 + nab], ((0, 0), (0, AB_PAD - nab)))],
        axis=1).astype(BF16)
    pad_row = lambda v: jnp.pad(v, (0, AB_PAD - GDN_HEADS)).reshape(1, AB_PAD)
    ii = jnp.arange(TM_IN)
    tri = ((ii[:, None] >= ii[None, :]) & ((ii[:, None] // CHUNK) == (ii[None, :] // CHUNK))).astype(BF16)

    q, k, v, zs, ga, ybg, gbc, gbr = _in_call(
        x, mod3, norm1_w, w_all, gdn_conv_w[0], sc_conv_w[0], pad_row(gdn_a_log[0]), pad_row(gdn_dt_bias[0]),
        w_sc_out[0].astype(BF16), tri)

    wg = w_gdn_proj[0].astype(BF16)
    x1 = _gdn_call(q, k, v, zs, gbc, gbr, ga, ybg, x, mod3, gdn_norm_w, wg, w_o[0].astype(BF16))

    return _ffn_call(x1, mod3, modf3, norm2_w, normf_w.reshape(1, d), w_ffn_in[0].astype(BF16), w_ffn_out[0].astype(BF16))
```

```python
import functools

import jax
import jax.numpy as jnp
from jax import lax
from jax.experimental import pallas as pl
from jax.experimental.pallas import tpu as pltpu

D_MODEL = 1024
GDN_HEADS = 8
HEAD_DIM = 128
GDN_CONV = 4
SC_CONV = 3
CHUNK = 64
EPS = 1e-6
N_MOD = 6
D_FF = 2816
AB_PAD = 128

C_Q, C_K, C_V, C_Z, C_SCB, C_SCC, C_SCX, C_GA, C_GB, C_AB = (
    0, 1024, 2048, 3072, 4096, 5120, 6144, 7168, 8192, 9216)
W_ALL_COLS = C_AB + AB_PAD

TM_IN = 256
TM_GDN = 128
GDN_SEQS = 4
TM_FFN = 512
FF_SLAB = 256
IN_SLAB = 256
CARRY = 8

BF16 = jnp.bfloat16
F32 = jnp.float32
MIB = 1024 * 1024


def _mm(a, b):
    return jnp.dot(a.astype(BF16), b.astype(BF16), preferred_element_type=F32)


def _mm_nt(a, b):
    return lax.dot_general(a.astype(BF16), b.astype(BF16), (((1,), (1,)), ((), ())),
                           preferred_element_type=F32)


def _mm_tn(a, b):
    return lax.dot_general(a.astype(BF16), b.astype(BF16), (((0,), (0,)), ((), ())),
                           preferred_element_type=F32)


def _bmm(a, b):
    return lax.dot_general(a.astype(BF16), b.astype(BF16), (((2,), (1,)), ((0,), (0,))),
                           preferred_element_type=F32)


def _bmm_nt(a, b):
    return lax.dot_general(a.astype(BF16), b.astype(BF16), (((2,), (2,)), ((0,), (0,))),
                           preferred_element_type=F32)


def _bmm_tn(a, b):
    return lax.dot_general(a.astype(BF16), b.astype(BF16), (((1,), (1,)), ((0,), (0,))),
                           preferred_element_type=F32)


def _silu(x):
    return x * jax.nn.sigmoid(x)


def _rms(x):
    return x * lax.rsqrt(jnp.mean(x * x, axis=-1, keepdims=True) + EPS)


def _mod_kernel(c_ref, w_ref, b_ref, o_ref):
    o_ref[...] = _mm(_silu(c_ref[...]), w_ref[...]) + b_ref[...]


def _mod_call(c, w, b):
    bsz, d = c.shape
    n = w.shape[1]
    tn = 1024
    return pl.pallas_call(
        _mod_kernel,
        grid=(n // tn,),
        in_specs=[pl.BlockSpec((bsz, d), lambda j: (0, 0)),
                  pl.BlockSpec((d, tn), lambda j: (0, j)),
                  pl.BlockSpec((1, tn), lambda j: (0, j))],
        out_specs=pl.BlockSpec((bsz, tn), lambda j: (0, j)),
        out_shape=jax.ShapeDtypeStruct((bsz, n), F32),
        compiler_params=pltpu.CompilerParams(dimension_semantics=("arbitrary",),
                                             vmem_limit_bytes=24 * MIB),
        name="mod_rows",
    )(c, w, b.reshape(1, n))


def _in_kernel(x_ref, mod_ref, n1_ref, w_ref, cw_ref, scw_ref, alog_ref, dtb_ref, wsc_ref, tri_ref,
               q_ref, k_ref, v_ref, zs_ref, ga_ref, ybg_ref, gbc_ref, gbr_ref, pbuf, ubuf):
    t = pl.program_id(1)
    tm = TM_IN

    @pl.when(t == 0)
    def _():
        pbuf[:, 0:CARRY, :] = jnp.zeros((4, CARRY, D_MODEL), F32)

    x = x_ref[0]
    sh1 = mod_ref[0, 0:1, :]
    sc1 = mod_ref[0, 1:2, :]
    h = (_rms(x) * n1_ref[...] * (1.0 + sc1) + sh1).astype(BF16)

    sw = IN_SLAB
    heads_per_slab = sw // HEAD_DIM

    def proj(c0, width=IN_SLAB):
        return jnp.dot(h, w_ref[:, c0:c0 + width], preferred_element_type=F32)

    def causal_conv(slot, col, p, taps, width):
        pbuf[slot, CARRY:CARRY + tm, col:col + sw] = p
        acc = taps[width - 1:width, :] * p
        for j in range(width - 1):
            sh = width - 1 - j
            acc = acc + taps[j:j + 1, :] * pbuf[slot, CARRY - sh:CARRY - sh + tm, col:col + sw]
        pbuf[slot, 0:CARRY, col:col + sw] = pbuf[slot, tm:tm + CARRY, col:col + sw]
        return acc

    def gdn_input(slot, c0, col, out_ref, norm, scale):
        taps = cw_ref[:, c0 + col:c0 + col + sw]
        y = _silu(causal_conv(slot, col, proj(c0 + col), taps, GDN_CONV))
        for i in range(heads_per_slab):
            yh = y[:, i * HEAD_DIM:(i + 1) * HEAD_DIM]
            if norm:
                yh = yh * lax.rsqrt(jnp.sum(yh * yh, axis=-1, keepdims=True) + EPS)
                if scale != 1.0:
                    yh = yh * scale
            out_ref[0, col // HEAD_DIM + i] = yh.astype(out_ref.dtype)

    def z_gate(col):
        zs = _silu(proj(C_Z + col))
        for i in range(heads_per_slab):
            zs_ref[0, col // HEAD_DIM + i] = zs[:, i * HEAD_DIM:(i + 1) * HEAD_DIM].astype(zs_ref.dtype)

    def decay_beta():
        pab = proj(C_AB, AB_PAD)
        lane = lax.broadcasted_iota(jnp.int32, (tm, AB_PAD), 1)
        xg = pab + dtb_ref[...]
        softplus = jnp.maximum(xg, 0.0) + jnp.log1p(jnp.exp(-jnp.abs(xg)))
        g = -jnp.exp(alog_ref[...]) * softplus
        gb = jnp.where(lane < GDN_HEADS, g, jax.nn.sigmoid(pab))
        g1 = gb.astype(BF16)
        r1 = gb - g1.astype(F32)
        g2 = r1.astype(BF16)
        g3 = (r1 - g2.astype(F32)).astype(BF16)
        tri = tri_ref[...]
        cum = (jnp.dot(tri, g1, preferred_element_type=F32) + jnp.dot(tri, g2, preferred_element_type=F32)
               + jnp.dot(tri, g3, preferred_element_type=F32))
        gbc = jnp.where(lane < GDN_HEADS, cum, gb)
        gbc_ref[0] = gbc
        gbr_ref[0] = jnp.transpose(gbc)[0:2 * GDN_HEADS, :]

    def short_conv_in(col):
        prod = proj(C_SCC + col) * proj(C_SCX + col)
        conv = causal_conv(3, col, prod, scw_ref[:, col:col + sw], SC_CONV)
        ubuf[:, col:col + sw] = (proj(C_SCB + col) * conv).astype(BF16)

    def short_conv_out(col):
        yb = jnp.dot(ubuf[...], wsc_ref[:, col:col + sw], preferred_element_type=F32)
        ybg_ref[0, :, col:col + sw] = (jax.nn.sigmoid(proj(C_GB + col)) * yb).astype(ybg_ref.dtype)

    def gate_a(col):
        ga_ref[0, :, col:col + sw] = jax.nn.sigmoid(proj(C_GA + col)).astype(ga_ref.dtype)

    cols = range(0, D_MODEL, sw)
    for col in cols:
        gdn_input(0, C_Q, col, q_ref, True, HEAD_DIM ** -0.5)
        short_conv_in(col)
    for col in cols:
        gdn_input(1, C_K, col, k_ref, True, 1.0)
        gate_a(col)
        z_gate(col)
    for col in cols:
        gdn_input(2, C_V, col, v_ref, False, 1.0)
        short_conv_out(col)
    decay_beta()


def _in_call(x, mod3, norm1_w, w_all, conv_w, sc_conv_w, alog_row, dtb_row, w_sc_out, tri):
    bsz, seq, d = x.shape
    tm = TM_IN
    nt = seq // tm
    const2 = lambda b, t: (0, 0)
    resident = dict(pipeline_mode=pl.Buffered(1))
    head_spec = pl.BlockSpec((1, GDN_HEADS, tm, HEAD_DIM), lambda b, t: (b, 0, t, 0))
    tok_spec = pl.BlockSpec((1, tm, d), lambda b, t: (b, t, 0))
    head_shape = jax.ShapeDtypeStruct((bsz, GDN_HEADS, seq, HEAD_DIM), BF16)
    return pl.pallas_call(
        _in_kernel,
        grid=(bsz, nt),
        in_specs=[tok_spec,
                  pl.BlockSpec((1, N_MOD, d), lambda b, t: (b, 0, 0)),
                  pl.BlockSpec((1, d), const2),
                  pl.BlockSpec((d, W_ALL_COLS), const2, **resident),
                  pl.BlockSpec((GDN_CONV, 3 * d), const2),
                  pl.BlockSpec((SC_CONV, d), const2),
                  pl.BlockSpec((1, AB_PAD), const2),
                  pl.BlockSpec((1, AB_PAD), const2),
                  pl.BlockSpec((d, d), const2, **resident),
                  pl.BlockSpec((tm, tm), const2)],
        out_specs=[head_spec, head_spec, head_spec, head_spec, tok_spec, tok_spec,
                   pl.BlockSpec((1, tm, AB_PAD), lambda b, t: (b, t, 0)),
                   pl.BlockSpec((1, 2 * GDN_HEADS, tm), lambda b, t: (b, 0, t))],
        out_shape=[head_shape, head_shape, head_shape, head_shape,
                   jax.ShapeDtypeStruct((bsz, seq, d), BF16),
                   jax.ShapeDtypeStruct((bsz, seq, d), BF16),
                   jax.ShapeDtypeStruct((bsz, seq, AB_PAD), F32),
                   jax.ShapeDtypeStruct((bsz, 2 * GDN_HEADS, seq), F32)],
        scratch_shapes=[pltpu.VMEM((4, CARRY + tm, d), F32), pltpu.VMEM((tm, d), BF16)],
        compiler_params=pltpu.CompilerParams(dimension_semantics=("arbitrary", "arbitrary"),
                                             vmem_limit_bytes=52 * MIB),
        name="in_proj",
    )(x, mod3, norm1_w, w_all, conv_w, sc_conv_w, alog_row, dtb_row, w_sc_out, tri)


def _block_diag(x, left):
    zero = jnp.zeros_like(x)
    return jnp.concatenate([jnp.where(left, x, zero), jnp.where(left, zero, x)], axis=1)


def _tri_inverse(low, left, blk16, blk32, eye):
    def mm(x, y):
        return _bmm(x, _block_diag(y.astype(BF16), left))

    a1 = jnp.where(blk16, -low, 0.0)
    p = eye + a1
    a2 = mm(a1, a1)
    a4 = mm(a2, a2)
    p = p + mm(p, a2)
    a8 = mm(a4, a4)
    p = p + mm(p, a4)
    p = p + mm(p, a8)
    o1 = jnp.where(jnp.logical_and(blk32, jnp.logical_not(blk16)), low, 0.0)
    p = p - mm(mm(p, o1), p)
    o2 = jnp.where(blk32, 0.0, low)
    p = p - mm(mm(p, o2), p)
    return p


def _gdn_kernel(q_ref, k_ref, v_ref, zs_ref, gbc_ref, gbr_ref, ga_ref, ybg_ref, x_ref, mod_ref,
                nw_ref, wg_ref, wo_ref, out_ref, state_ref):
    t = pl.program_id(1)
    tm = TM_GDN
    nchunk = tm // CHUNK
    pair = 2 * CHUNK
    npair = tm // pair
    ns = GDN_SEQS
    nh = GDN_HEADS
    nhs = ns * nh
    nb = nhs * npair

    @pl.when(t == 0)
    def _():
        state_ref[...] = jnp.zeros_like(state_ref)

    gbc = gbc_ref[...]
    gbr = gbr_ref[...]

    def col_heads(j0):
        return jnp.stack([jnp.broadcast_to(gbc[sq, :, j0 + h:j0 + h + 1], (tm, HEAD_DIM))
                          for sq in range(ns) for h in range(nh)]).reshape(nb, pair, HEAD_DIM)

    def row_heads(j0):
        return jnp.stack([jnp.broadcast_to(gbr[sq, j0 + h:j0 + h + 1, m * pair:(m + 1) * pair], (CHUNK, pair))
                          for sq in range(ns) for h in range(nh) for m in range(npair)])

    gcb = col_heads(0)
    ab = jnp.exp(gcb)
    gc4 = gcb.reshape(nb * 2, CHUNK, HEAD_DIM)
    eb = jnp.exp(gc4[:, CHUNK - 1:CHUNK, :] - gc4).reshape(nb, pair, HEAD_DIM)
    grow = row_heads(0)
    brow = row_heads(nh)
    barow = brow * jnp.exp(grow)

    ri = lax.broadcasted_iota(jnp.int32, (nb, CHUNK, pair), 1)
    li = lax.broadcasted_iota(jnp.int32, (nb, CHUNK, pair), 2)
    left = li < CHUNK
    ci = jnp.where(left, li, li - CHUNK)
    causal = ri >= ci
    strict = ri > ci
    blk16 = (ri // 16) == (ci // 16)
    blk32 = (ri // 32) == (ci // 32)
    eye = jnp.where(ri == ci, 1.0, 0.0).astype(F32)

    def side_by_side(g):
        return jnp.where(left, g[:, 0:CHUNK, :], g[:, CHUNK:pair, :])

    q = q_ref[...].reshape(nb, pair, HEAD_DIM)
    k = k_ref[...].reshape(nb, pair, HEAD_DIM)
    v = v_ref[...].reshape(nb, pair, HEAD_DIM)

    def chunks_on_lanes(x):
        return jnp.concatenate([x[:, 0:CHUNK], x[:, CHUNK:pair]], axis=2)

    first = lax.broadcasted_iota(jnp.int32, (nb, pair, HEAD_DIM), 1) < CHUNK
    k_zero = jnp.zeros_like(k)
    k_diag = jnp.concatenate([jnp.where(first, k, k_zero), jnp.where(first, k_zero, k)], axis=2)

    dec = jnp.where(causal, jnp.exp(jnp.minimum(side_by_side(gcb) - grow, 0.0)), 0.0)
    bcol = side_by_side(col_heads(nh))
    low = jnp.where(strict, _bmm_nt(chunks_on_lanes(k), k_diag) * bcol * dec, 0.0)
    qk = jnp.where(causal, _bmm_nt(chunks_on_lanes(q), k_diag) * dec, 0.0)
    tinv = _tri_inverse(low, left, blk16, blk32, eye)
    tb = tinv * brow
    u = _bmm(_block_diag(tb.astype(BF16), left), v)
    w = _bmm(_block_diag((tinv * barow).astype(BF16), left), k)

    def chunk_rows(arr, c):
        r0 = (c % 2) * CHUNK
        return arr.reshape((nhs, npair) + arr.shape[1:])[:, c // 2, r0:r0 + CHUNK]

    s = state_ref[...]
    zeros = jnp.zeros((nhs, CHUNK, HEAD_DIM), F32)
    outs = []
    for c in range(nchunk):
        ab_c = chunk_rows(ab, c)
        ws_qs = _bmm(jnp.concatenate([chunk_rows(w, c).astype(BF16), chunk_rows(q, c)], axis=1), s)
        v_new = chunk_rows(u, c) - ws_qs[:, 0:CHUNK]
        v_pad = jnp.concatenate([v_new, zeros] if c % 2 == 0 else [zeros, v_new], axis=1)
        qk_c = qk.reshape(nhs, npair, CHUNK, pair)[:, c // 2]
        outs.append(ab_c * ws_qs[:, CHUNK:] + _bmm(qk_c, v_pad))
        s = s * ab_c[:, CHUNK - 1:CHUNK, :] + _bmm_tn(chunk_rows(k, c), chunk_rows(eb, c) * v_new)
    state_ref[...] = s

    o = jnp.concatenate(outs, axis=1)
    on = (_rms(o) * nw_ref[...] * zs_ref[...].reshape(nhs, tm, HEAD_DIM).astype(F32)).astype(BF16)
    o2d = jnp.concatenate([jnp.concatenate([on[sq * nh + h] for h in range(nh)], axis=1)
                           for sq in range(ns)], axis=0)
    ya = jnp.dot(o2d, wg_ref[...], preferred_element_type=F32)

    d = ya.shape[1]
    merged = ga_ref[...].reshape(ns * tm, d).astype(F32) * ya + ybg_ref[...].reshape(ns * tm, d).astype(F32)
    mix = _mm(merged, wo_ref[...])
    for sq in range(ns):
        out_ref[sq] = x_ref[sq] + mod_ref[sq, 2:3, :] * mix[sq * tm:(sq + 1) * tm]


def _gdn_call(q, k, v, zs, gbc, gbr, ga, ybg, x, mod3, gdn_norm_w, wg, wo):
    bsz, seq, d = x.shape
    tm = TM_GDN
    ns = GDN_SEQS
    nt = seq // tm
    const2 = lambda b, t: (0, 0)
    resident = dict(pipeline_mode=pl.Buffered(1))
    head_spec = pl.BlockSpec((ns, GDN_HEADS, tm, HEAD_DIM), lambda b, t: (b, 0, t, 0))
    tok_spec = pl.BlockSpec((ns, tm, d), lambda b, t: (b, t, 0))
    return pl.pallas_call(
        _gdn_kernel,
        grid=(bsz // ns, nt),
        in_specs=[head_spec, head_spec, head_spec, head_spec,
                  pl.BlockSpec((ns, tm, AB_PAD), lambda b, t: (b, t, 0)),
                  pl.BlockSpec((ns, 2 * GDN_HEADS, tm), lambda b, t: (b, 0, t)),
                  tok_spec, tok_spec, tok_spec,
                  pl.BlockSpec((ns, N_MOD, d), lambda b, t: (b, 0, 0)),
                  pl.BlockSpec((1, HEAD_DIM), const2),
                  pl.BlockSpec((d, d), const2, **resident),
                  pl.BlockSpec((d, d), const2, **resident)],
        out_specs=tok_spec,
        out_shape=jax.ShapeDtypeStruct((bsz, seq, d), F32),
        scratch_shapes=[pltpu.VMEM((ns * GDN_HEADS, HEAD_DIM, HEAD_DIM), F32)],
        compiler_params=pltpu.CompilerParams(dimension_semantics=("arbitrary", "arbitrary"),
                                             vmem_limit_bytes=48 * MIB),
        name="gdn_mix",
    )(q, k, v, zs, gbc, gbr, ga, ybg, x, mod3, gdn_norm_w, wg, wo)


def _ffn_kernel(x_ref, mod_ref, modf_ref, n2_ref, nf_ref, win_ref, wout_ref, out_ref):
    x = x_ref[0]
    sh2 = mod_ref[0, 3:4, :]
    sc2 = mod_ref[0, 4:5, :]
    g2 = mod_ref[0, 5:6, :]
    h = (_rms(x) * n2_ref[...] * (1.0 + sc2) + sh2).astype(BF16)
    acc = jnp.zeros(x.shape, F32)
    for s0 in range(0, D_FF, FF_SLAB):
        gate = jnp.dot(h, win_ref[:, s0:s0 + FF_SLAB], preferred_element_type=F32)
        up = jnp.dot(h, win_ref[:, D_FF + s0:D_FF + s0 + FF_SLAB], preferred_element_type=F32)
        act = (_silu(gate) * up).astype(BF16)
        acc = acc + jnp.dot(act, wout_ref[s0:s0 + FF_SLAB, :], preferred_element_type=F32)
    x2 = x + g2 * acc
    out_ref[0] = _rms(x2) * nf_ref[...] * (1.0 + modf_ref[0, 1:2, :]) + modf_ref[0, 0:1, :]


def _ffn_call(x, mod3, modf3, norm2_w, normf_w, w_in, w_out):
    bsz, seq, d = x.shape
    tm = TM_FFN
    const2 = lambda b, t: (0, 0)
    resident = dict(pipeline_mode=pl.Buffered(1))
    tok_spec = pl.BlockSpec((1, tm, d), lambda b, t: (b, t, 0))
    return pl.pallas_call(
        _ffn_kernel,
        grid=(bsz, seq // tm),
        in_specs=[tok_spec,
                  pl.BlockSpec((1, N_MOD, d), lambda b, t: (b, 0, 0)),
                  pl.BlockSpec((1, 2, d), lambda b, t: (b, 0, 0)),
                  pl.BlockSpec((1, d), const2),
                  pl.BlockSpec((1, d), const2),
                  pl.BlockSpec((d, 2 * D_FF), const2, **resident),
                  pl.BlockSpec((D_FF, d), const2, **resident)],
        out_specs=tok_spec,
        out_shape=jax.ShapeDtypeStruct((bsz, seq, d), F32),
        compiler_params=pltpu.CompilerParams(dimension_semantics=("arbitrary", "arbitrary"),
                                             vmem_limit_bytes=48 * MIB),
        name="ffn_final",
    )(x, mod3, modf3, norm2_w, normf_w, w_in, w_out)


def kernel(x, c, w_ada, b_ada, norm1_w, w_in, gdn_conv_w, gdn_a_log, gdn_dt_bias, gdn_norm_w, w_gdn_proj,
           sc_conv_w, w_sc_out, w_o, norm2_w, w_ffn_in, w_ffn_out, w_ada_f, b_ada_f, normf_w):
    bsz, seq, d = x.shape
    assert w_ada.shape[0] == 1 and d == D_MODEL and seq % TM_FFN == 0
    ab0 = C_Z + D_MODEL
    nab = 2 * GDN_HEADS

    mod3 = _mod_call(c, w_ada[0], b_ada[0]).reshape(bsz, N_MOD, d)
    modf3 = _mod_call(c, w_ada_f, b_ada_f).reshape(bsz, 2, d)

    w_l = w_in[0]
    w_all = jnp.concatenate(
        [w_l[:, :ab0], w_l[:, ab0 + nab:], jnp.pad(w_l[:, ab0:ab0 + nab], ((0, 0), (0, AB_PAD - nab)))],
        axis=1).astype(BF16)
    pad_row = lambda v: jnp.pad(v, (0, AB_PAD - GDN_HEADS)).reshape(1, AB_PAD)
    ii = jnp.arange(TM_IN)
    tri = ((ii[:, None] >= ii[None, :]) & ((ii[:, None] // CHUNK) == (ii[None, :] // CHUNK))).astype(BF16)

    q, k, v, zs, ga, ybg, gbc, gbr = _in_call(
        x, mod3, norm1_w, w_all, gdn_conv_w[0], sc_conv_w[0], pad_row(gdn_a_log[0]), pad_row(gdn_dt_bias[0]),
        w_sc_out[0].astype(BF16), tri)

    wg = w_gdn_proj[0].astype(BF16)
    x1 = _gdn_call(q, k, v, zs, gbc, gbr, ga, ybg, x, mod3, gdn_norm_w, wg, w_o[0].astype(BF16))

    return _ffn_call(x1, mod3, modf3, norm2_w, normf_w.reshape(1, d), w_ffn_in[0].astype(BF16), w_ffn_out[0].astype(BF16))
```

```python
import functools

import jax
import jax.numpy as jnp
from jax import lax
from jax.experimental import pallas as pl
from jax.experimental.pallas import tpu as pltpu

D_MODEL = 1024
GDN_HEADS = 8
HEAD_DIM = 128
GDN_CONV = 4
SC_CONV = 3
CHUNK = 64
EPS = 1e-6
N_MOD = 6
D_FF = 2816
AB_PAD = 128

C_Q, C_K, C_V, C_Z, C_SCB, C_SCC, C_SCX, C_GA, C_GB, C_AB = (
    0, 1024, 2048, 3072, 4096, 5120, 6144, 7168, 8192, 9216)
W_ALL_COLS = C_AB + AB_PAD

TM_IN = 256
TM_GDN = 128
GDN_SEQS = 4
GDN_TILES = 16
GDN_SLAB = 512
TM_FFN = 512
FF_SLAB = 256
IN_SLAB = 256
CARRY = 8

BF16 = jnp.bfloat16
F32 = jnp.float32
MIB = 1024 * 1024


def _mm(a, b):
    return jnp.dot(a.astype(BF16), b.astype(BF16), preferred_element_type=F32)


def _mm_nt(a, b):
    return lax.dot_general(a.astype(BF16), b.astype(BF16), (((1,), (1,)), ((), ())),
                           preferred_element_type=F32)


def _mm_tn(a, b):
    return lax.dot_general(a.astype(BF16), b.astype(BF16), (((0,), (0,)), ((), ())),
                           preferred_element_type=F32)


def _bmm(a, b):
    return lax.dot_general(a.astype(BF16), b.astype(BF16), (((2,), (1,)), ((0,), (0,))),
                           preferred_element_type=F32)


def _bmm_nt(a, b):
    return lax.dot_general(a.astype(BF16), b.astype(BF16), (((2,), (2,)), ((0,), (0,))),
                           preferred_element_type=F32)


def _bmm_tn(a, b):
    return lax.dot_general(a.astype(BF16), b.astype(BF16), (((1,), (1,)), ((0,), (0,))),
                           preferred_element_type=F32)


def _silu(x):
    return x * jax.nn.sigmoid(x)


def _rms(x):
    return x * lax.rsqrt(jnp.mean(x * x, axis=-1, keepdims=True) + EPS)


def _mod_kernel(c_ref, w_ref, b_ref, o_ref):
    o_ref[...] = _mm(_silu(c_ref[...]), w_ref[...]) + b_ref[...]


def _mod_call(c, w, b):
    bsz, d = c.shape
    n = w.shape[1]
    tn = 1024
    return pl.pallas_call(
        _mod_kernel,
        grid=(n // tn,),
        in_specs=[pl.BlockSpec((bsz, d), lambda j: (0, 0)),
                  pl.BlockSpec((d, tn), lambda j: (0, j)),
                  pl.BlockSpec((1, tn), lambda j: (0, j))],
        out_specs=pl.BlockSpec((bsz, tn), lambda j: (0, j)),
        out_shape=jax.ShapeDtypeStruct((bsz, n), F32),
        compiler_params=pltpu.CompilerParams(dimension_semantics=("arbitrary",),
                                             vmem_limit_bytes=24 * MIB),
        name="mod_rows",
    )(c, w, b.reshape(1, n))


def _in_kernel(x_ref, mod_ref, n1_ref, w_ref, cw_ref, scw_ref, alog_ref, dtb_ref, wsc_ref, tri_ref,
               q_ref, k_ref, v_ref, zs_ref, ga_ref, ybg_ref, gbc_ref, gbr_ref, carry, pbuf, ubuf):
    t = pl.program_id(1)
    tm = TM_IN

    @pl.when(t == 0)
    def _():
        carry[...] = jnp.zeros_like(carry)

    x = x_ref[0]
    sh1 = mod_ref[0, 0:1, :]
    sc1 = mod_ref[0, 1:2, :]
    h = (_rms(x) * n1_ref[...] * (1.0 + sc1) + sh1).astype(BF16)

    sw = IN_SLAB
    heads_per_slab = sw // HEAD_DIM

    def proj(c0, width=IN_SLAB):
        return jnp.dot(h, w_ref[:, c0:c0 + width], preferred_element_type=F32)

    def causal_conv(slot, col, p, taps, width):
        pbuf[slot, 0:CARRY, :] = carry[slot, :, col:col + sw]
        pbuf[slot, CARRY:CARRY + tm, :] = p
        acc = taps[width - 1:width, :] * p
        for j in range(width - 1):
            sh = width - 1 - j
            acc = acc + taps[j:j + 1, :] * pbuf[slot, CARRY - sh:CARRY - sh + tm, :]
        carry[slot, :, col:col + sw] = p[tm - CARRY:tm]
        return acc

    def gdn_input(slot, c0, col, out_ref, norm, scale):
        taps = cw_ref[:, c0 + col:c0 + col + sw]
        y = _silu(causal_conv(slot, col, proj(c0 + col), taps, GDN_CONV))
        for i in range(heads_per_slab):
            yh = y[:, i * HEAD_DIM:(i + 1) * HEAD_DIM]
            if norm:
                yh = yh * lax.rsqrt(jnp.sum(yh * yh, axis=-1, keepdims=True) + EPS)
                if scale != 1.0:
                    yh = yh * scale
            out_ref[0, col // HEAD_DIM + i] = yh.astype(out_ref.dtype)

    def z_gate(col):
        zs = _silu(proj(C_Z + col))
        for i in range(heads_per_slab):
            zs_ref[0, col // HEAD_DIM + i] = zs[:, i * HEAD_DIM:(i + 1) * HEAD_DIM].astype(zs_ref.dtype)

    def decay_beta():
        pab = proj(C_AB, AB_PAD)
        lane = lax.broadcasted_iota(jnp.int32, (tm, AB_PAD), 1)
        xg = pab + dtb_ref[...]
        softplus = jnp.maximum(xg, 0.0) + jnp.log1p(jnp.exp(-jnp.abs(xg)))
        g = -jnp.exp(alog_ref[...]) * softplus
        gb = jnp.where(lane < GDN_HEADS, g, jax.nn.sigmoid(pab))
        g1 = gb.astype(BF16)
        r1 = gb - g1.astype(F32)
        g2 = r1.astype(BF16)
        g3 = (r1 - g2.astype(F32)).astype(BF16)
        tri = tri_ref[...]
        cum = (jnp.dot(tri, g1, preferred_element_type=F32) + jnp.dot(tri, g2, preferred_element_type=F32)
               + jnp.dot(tri, g3, preferred_element_type=F32))
        gbc = jnp.where(lane < GDN_HEADS, cum, gb)
        gbc_ref[0] = gbc
        gbr_ref[0] = jnp.transpose(gbc)[0:2 * GDN_HEADS, :]

    def short_conv_in(col):
        prod = proj(C_SCC + col) * proj(C_SCX + col)
        conv = causal_conv(3, col, prod, scw_ref[:, col:col + sw], SC_CONV)
        ubuf[:, col:col + sw] = (proj(C_SCB + col) * conv).astype(BF16)

    def short_conv_out(col):
        yb = jnp.dot(ubuf[...], wsc_ref[:, col:col + sw], preferred_element_type=F32)
        ybg_ref[0, :, col:col + sw] = (jax.nn.sigmoid(proj(C_GB + col)) * yb).astype(ybg_ref.dtype)

    def gate_a(col):
        ga_ref[0, :, col:col + sw] = jax.nn.sigmoid(proj(C_GA + col)).astype(ga_ref.dtype)

    cols = range(0, D_MODEL, sw)
    for col in cols:
        gdn_input(0, C_Q, col, q_ref, True, HEAD_DIM ** -0.5)
        short_conv_in(col)
    for col in cols:
        gdn_input(1, C_K, col, k_ref, True, 1.0)
        gate_a(col)
        z_gate(col)
    for col in cols:
        gdn_input(2, C_V, col, v_ref, False, 1.0)
        short_conv_out(col)
    decay_beta()


def _in_call(x, mod3, norm1_w, w_all, conv_w, sc_conv_w, alog_row, dtb_row, w_sc_out, tri):
    bsz, seq, d = x.shape
    tm = TM_IN
    nt = seq // tm
    const2 = lambda b, t: (0, 0)
    resident = dict(pipeline_mode=pl.Buffered(1))
    head_spec = pl.BlockSpec((1, GDN_HEADS, tm, HEAD_DIM), lambda b, t: (b, 0, t, 0))
    tok_spec = pl.BlockSpec((1, tm, d), lambda b, t: (b, t, 0))
    head_shape = jax.ShapeDtypeStruct((bsz, GDN_HEADS, seq, HEAD_DIM), BF16)
    return pl.pallas_call(
        _in_kernel,
        grid=(bsz, nt),
        in_specs=[tok_spec,
                  pl.BlockSpec((1, N_MOD, d), lambda b, t: (b, 0, 0)),
                  pl.BlockSpec((1, d), const2),
                  pl.BlockSpec((d, W_ALL_COLS), const2, **resident),
                  pl.BlockSpec((GDN_CONV, 3 * d), const2),
                  pl.BlockSpec((SC_CONV, d), const2),
                  pl.BlockSpec((1, AB_PAD), const2),
                  pl.BlockSpec((1, AB_PAD), const2),
                  pl.BlockSpec((d, d), const2, **resident),
                  pl.BlockSpec((tm, tm), const2)],
        out_specs=[head_spec, head_spec, head_spec, head_spec, tok_spec, tok_spec,
                   pl.BlockSpec((1, tm, AB_PAD), lambda b, t: (b, t, 0)),
                   pl.BlockSpec((1, 2 * GDN_HEADS, tm), lambda b, t: (b, 0, t))],
        out_shape=[head_shape, head_shape, head_shape, head_shape,
                   jax.ShapeDtypeStruct((bsz, seq, d), BF16),
                   jax.ShapeDtypeStruct((bsz, seq, d), BF16),
                   jax.ShapeDtypeStruct((bsz, seq, AB_PAD), F32),
                   jax.ShapeDtypeStruct((bsz, 2 * GDN_HEADS, seq), F32)],
        scratch_shapes=[pltpu.VMEM((4, CARRY, d), F32),
                        pltpu.VMEM((4, CARRY + tm, IN_SLAB), F32),
                        pltpu.VMEM((tm, d), BF16)],
        compiler_params=pltpu.CompilerParams(dimension_semantics=("arbitrary", "arbitrary"),
                                             vmem_limit_bytes=56 * MIB),
        name="in_proj",
    )(x, mod3, norm1_w, w_all, conv_w, sc_conv_w, alog_row, dtb_row, w_sc_out, tri)


def _block_diag(x, left):
    zero = jnp.zeros_like(x)
    return jnp.concatenate([jnp.where(left, x, zero), jnp.where(left, zero, x)], axis=1)


def _tri_inverse_stages(low, left, blk16, blk32, eye):
    def mm(x, y):
        return _bmm(x, _block_diag(y.astype(BF16), left))

    a1 = jnp.where(blk16, -low, 0.0)
    p = eye + a1
    sq = mm(a1, a1)
    yield
    for _ in range(2):
        both = mm(jnp.concatenate([p, sq], axis=1), sq)
        yield
        p = p + both[:, 0:CHUNK]
        sq = both[:, CHUNK:2 * CHUNK]
    p = p + mm(p, sq)
    yield
    for off in (jnp.where(jnp.logical_and(blk32, jnp.logical_not(blk16)), low, 0.0),
                jnp.where(blk32, 0.0, low)):
        po = mm(p, off)
        yield
        p = p - mm(po, p)
        yield
    return p


def _gdn_kernel(qn_ref, kn_ref, vn_ref, gbc_ref, gbr_ref,
                qp_ref, kp_ref, zs_ref, ga_ref, ybg_ref, x_ref, mod_ref, nw_ref, wg_ref, wo_ref,
                out_ref, state_ref, u_scr, w_scr, qk_scr, ab_scr, eb_scr, dl_scr):
    i = pl.program_id(0)
    tm = TM_GDN
    ns = GDN_SEQS
    nh = GDN_HEADS
    nb = ns * nh
    d = D_MODEL
    slot_w = lax.rem(i, 2)
    slot_r = 1 - slot_w
    t_fin = lax.rem(jnp.maximum(i - 1, 0), GDN_TILES)

    @pl.when(i == 0)
    def _():
        u_scr[1] = jnp.zeros(u_scr.shape[1:], u_scr.dtype)
        w_scr[1] = jnp.zeros(w_scr.shape[1:], w_scr.dtype)
        qk_scr[1] = jnp.zeros(qk_scr.shape[1:], qk_scr.dtype)
        ab_scr[1] = jnp.zeros(ab_scr.shape[1:], ab_scr.dtype)
        eb_scr[1] = jnp.zeros(eb_scr.shape[1:], eb_scr.dtype)
        dl_scr[1] = jnp.zeros(dl_scr.shape[1:], dl_scr.dtype)

    @pl.when(t_fin == 0)
    def _():
        state_ref[...] = jnp.zeros_like(state_ref)

    def prepare():
        gbc = gbc_ref[...]
        gbr = gbr_ref[...]

        def col_heads(j0):
            return jnp.stack([jnp.broadcast_to(gbc[sq, :, j0 + h:j0 + h + 1], (tm, HEAD_DIM))
                              for sq in range(ns) for h in range(nh)])

        def row_heads(j0):
            return jnp.stack([jnp.broadcast_to(gbr[sq, j0 + h:j0 + h + 1, :], (CHUNK, tm))
                              for sq in range(ns) for h in range(nh)])

        ri = lax.broadcasted_iota(jnp.int32, (nb, CHUNK, tm), 1)
        li = lax.broadcasted_iota(jnp.int32, (nb, CHUNK, tm), 2)
        left = li < CHUNK
        ci = jnp.where(left, li, li - CHUNK)
        causal = ri >= ci
        strict = ri > ci
        blk16 = (ri // 16) == (ci // 16)
        blk32 = (ri // 32) == (ci // 32)
        eye = jnp.where(ri == ci, 1.0, 0.0).astype(F32)

        def side_by_side(g):
            return jnp.where(left, g[:, 0:CHUNK, :], g[:, CHUNK:tm, :])

        q = qn_ref[...].reshape(nb, tm, HEAD_DIM)
        k = kn_ref[...].reshape(nb, tm, HEAD_DIM)
        v = vn_ref[...].reshape(nb, tm, HEAD_DIM)

        gkk = side_by_side(_bmm_nt(k, k))
        yield
        gqk = side_by_side(_bmm_nt(q, k))
        yield

        gcb = col_heads(0)
        ab = jnp.exp(gcb)
        gc2 = gcb.reshape(nb * 2, CHUNK, HEAD_DIM)
        ab_scr[slot_w] = ab.astype(ab_scr.dtype)
        eb_scr[slot_w] = jnp.exp(gc2[:, CHUNK - 1:CHUNK, :] - gc2).reshape(nb, tm, HEAD_DIM).astype(eb_scr.dtype)
        dl_scr[slot_w] = ab.reshape(nb, 2, CHUNK, HEAD_DIM)[:, :, CHUNK - 1, :]
        grow = row_heads(0)
        brow = row_heads(nh)
        barow = brow * jnp.exp(grow)

        dec = jnp.where(causal, jnp.exp(jnp.minimum(side_by_side(gcb) - grow, 0.0)), 0.0)
        bcol = side_by_side(col_heads(nh))
        low = jnp.where(strict, gkk * bcol * dec, 0.0)
        qk = jnp.where(causal, gqk * dec, 0.0).astype(qk_scr.dtype)
        qk_scr[slot_w, 0] = qk[:, :, 0:CHUNK]
        qk_scr[slot_w, 1] = qk[:, :, CHUNK:tm]
        tinv = yield from _tri_inverse_stages(low, left, blk16, blk32, eye)
        u_scr[slot_w] = _bmm(_block_diag((tinv * brow).astype(BF16), left), v)
        yield
        w_scr[slot_w] = _bmm(_block_diag((tinv * barow).astype(BF16), left), k).astype(w_scr.dtype)
        yield

    def finish():
        q = qp_ref[...].reshape(nb, tm, HEAD_DIM)
        k = kp_ref[...].reshape(nb, tm, HEAD_DIM)
        s = state_ref[...]
        outs = []
        for c in range(tm // CHUNK):
            rows = slice(c * CHUNK, (c + 1) * CHUNK)
            ws_qs = _bmm(jnp.concatenate([w_scr[slot_r, :, rows, :], q[:, rows]], axis=1), s)
            yield
            v_new = u_scr[slot_r, :, rows, :] - ws_qs[:, 0:CHUNK]
            outs.append(ab_scr[slot_r, :, rows, :].astype(F32) * ws_qs[:, CHUNK:]
                        + _bmm(qk_scr[slot_r, c], v_new))
            yield
            s = (s * dl_scr[slot_r, :, c:c + 1, :]
                 + _bmm_tn(k[:, rows], eb_scr[slot_r, :, rows, :].astype(F32) * v_new))
            yield
        state_ref[...] = s

        o = jnp.concatenate(outs, axis=1)
        on = (_rms(o) * nw_ref[...] * zs_ref[...].reshape(nb, tm, HEAD_DIM).astype(F32)).astype(BF16)
        o2d = jnp.concatenate([jnp.concatenate([on[sq * nh + h] for h in range(nh)], axis=1)
                               for sq in range(ns)], axis=0)
        ga = ga_ref[...].reshape(ns * tm, d)
        ybg = ybg_ref[...].reshape(ns * tm, d)
        merged = []
        for c0 in range(0, d, GDN_SLAB):
            ya = jnp.dot(o2d, wg_ref[:, c0:c0 + GDN_SLAB], preferred_element_type=F32)
            merged.append((ga[:, c0:c0 + GDN_SLAB].astype(F32) * ya
                           + ybg[:, c0:c0 + GDN_SLAB].astype(F32)).astype(BF16))
            yield
        merged = jnp.concatenate(merged, axis=1)
        for c0 in range(0, d, GDN_SLAB):
            mix = jnp.dot(merged, wo_ref[:, c0:c0 + GDN_SLAB], preferred_element_type=F32)
            for sq in range(ns):
                out_ref[sq, :, c0:c0 + GDN_SLAB] = (x_ref[sq, :, c0:c0 + GDN_SLAB]
                                                    + mod_ref[sq, 2:3, c0:c0 + GDN_SLAB] * mix[sq * tm:(sq + 1) * tm])
            yield

    active = [finish(), prepare()]
    while active:
        for stage in list(active):
            try:
                next(stage)
            except StopIteration:
                active.remove(stage)


def _gdn_call(q, k, v, zs, gbc, gbr, ga, ybg, x, mod3, gdn_norm_w, wg, wo):
    bsz, seq, d = x.shape
    tm = TM_GDN
    ns = GDN_SEQS
    nb = ns * GDN_HEADS
    assert seq // tm == GDN_TILES and tm == 2 * CHUNK and bsz % ns == 0
    last = (bsz // ns) * GDN_TILES - 1

    def tile(j):
        return j // GDN_TILES, lax.rem(j, GDN_TILES)

    def nxt(i):
        return tile(jnp.minimum(i, last))

    def fin(i):
        return tile(jnp.maximum(i - 1, 0))

    const2 = lambda i: (0, 0)
    resident = dict(pipeline_mode=pl.Buffered(1))
    head_blk = (ns, GDN_HEADS, tm, HEAD_DIM)
    head_nxt = pl.BlockSpec(head_blk, lambda i: (nxt(i)[0], 0, nxt(i)[1], 0))
    head_fin = pl.BlockSpec(head_blk, lambda i: (fin(i)[0], 0, fin(i)[1], 0))
    tok_fin = pl.BlockSpec((ns, tm, d), lambda i: (fin(i)[0], fin(i)[1], 0))
    return pl.pallas_call(
        _gdn_kernel,
        grid=(last + 2,),
        in_specs=[head_nxt, head_nxt, head_nxt,
                  pl.BlockSpec((ns, tm, AB_PAD), lambda i: (nxt(i)[0], nxt(i)[1], 0)),
                  pl.BlockSpec((ns, 2 * GDN_HEADS, tm), lambda i: (nxt(i)[0], 0, nxt(i)[1])),
                  head_fin, head_fin, head_fin, tok_fin, tok_fin, tok_fin,
                  pl.BlockSpec((ns, N_MOD, d), lambda i: (fin(i)[0], 0, 0)),
                  pl.BlockSpec((1, HEAD_DIM), const2),
                  pl.BlockSpec((d, d), const2, **resident),
                  pl.BlockSpec((d, d), const2, **resident)],
        out_specs=tok_fin,
        out_shape=jax.ShapeDtypeStruct((bsz, seq, d), F32),
        scratch_shapes=[pltpu.VMEM((nb, HEAD_DIM, HEAD_DIM), F32),
                        pltpu.VMEM((2, nb, tm, HEAD_DIM), F32),
                        pltpu.VMEM((2, nb, tm, HEAD_DIM), BF16),
                        pltpu.VMEM((2, 2, nb, CHUNK, CHUNK), BF16),
                        pltpu.VMEM((2, nb, tm, HEAD_DIM), BF16),
                        pltpu.VMEM((2, nb, tm, HEAD_DIM), BF16),
                        pltpu.VMEM((2, nb, 2, HEAD_DIM), F32)],
        compiler_params=pltpu.CompilerParams(dimension_semantics=("arbitrary",),
                                             vmem_limit_bytes=56 * MIB),
        name="gdn_mix",
    )(q, k, v, gbc, gbr, q, k, zs, ga, ybg, x, mod3, gdn_norm_w, wg, wo)


def _ffn_kernel(x_ref, mod_ref, modf_ref, n2_ref, nf_ref, win_ref, wout_ref, out_ref):
    x = x_ref[0]
    sh2 = mod_ref[0, 3:4, :]
    sc2 = mod_ref[0, 4:5, :]
    g2 = mod_ref[0, 5:6, :]
    h = (_rms(x) * n2_ref[...] * (1.0 + sc2) + sh2).astype(BF16)
    acc = jnp.zeros(x.shape, F32)
    for s0 in range(0, D_FF, FF_SLAB):
        gate = jnp.dot(h, win_ref[:, s0:s0 + FF_SLAB], preferred_element_type=F32)
        up = jnp.dot(h, win_ref[:, D_FF + s0:D_FF + s0 + FF_SLAB], preferred_element_type=F32)
        act = (_silu(gate) * up).astype(BF16)
        acc = acc + jnp.dot(act, wout_ref[s0:s0 + FF_SLAB, :], preferred_element_type=F32)
    x2 = x + g2 * acc
    out_ref[0] = _rms(x2) * nf_ref[...] * (1.0 + modf_ref[0, 1:2, :]) + modf_ref[0, 0:1, :]


def _ffn_call(x, mod3, modf3, norm2_w, normf_w, w_in, w_out):
    bsz, seq, d = x.shape
    tm = TM_FFN
    const2 = lambda b, t: (0, 0)
    resident = dict(pipeline_mode=pl.Buffered(1))
    tok_spec = pl.BlockSpec((1, tm, d), lambda b, t: (b, t, 0))
    return pl.pallas_call(
        _ffn_kernel,
        grid=(bsz, seq // tm),
        in_specs=[tok_spec,
                  pl.BlockSpec((1, N_MOD, d), lambda b, t: (b, 0, 0)),
                  pl.BlockSpec((1, 2, d), lambda b, t: (b, 0, 0)),
                  pl.BlockSpec((1, d), const2),
                  pl.BlockSpec((1, d), const2),
                  pl.BlockSpec((d, 2 * D_FF), const2, **resident),
                  pl.BlockSpec((D_FF, d), const2, **resident)],
        out_specs=tok_spec,
        out_shape=jax.ShapeDtypeStruct((bsz, seq, d), F32),
        compiler_params=pltpu.CompilerParams(dimension_semantics=("arbitrary", "arbitrary"),
                                             vmem_limit_bytes=48 * MIB),
        name="ffn_final",
    )(x, mod3, modf3, norm2_w, normf_w, w_in, w_out)


def kernel(x, c, w_ada, b_ada, norm1_w, w_in, gdn_conv_w, gdn_a_log, gdn_dt_bias, gdn_norm_w, w_gdn_proj,
           sc_conv_w, w_sc_out, w_o, norm2_w, w_ffn_in, w_ffn_out, w_ada_f, b_ada_f, normf_w):
    bsz, seq, d = x.shape
    assert w_ada.shape[0] == 1 and d == D_MODEL and seq % TM_FFN == 0
    ab0 = C_Z + D_MODEL
    nab = 2 * GDN_HEADS

    mod3 = _mod_call(c, w_ada[0], b_ada[0]).reshape(bsz, N_MOD, d)
    modf3 = _mod_call(c, w_ada_f, b_ada_f).reshape(bsz, 2, d)

    w_l = w_in[0].astype(BF16)
    w_all = jnp.concatenate(
        [w_l[:, :ab0], w_l[:, ab0 + nab:], jnp.pad(w_l[:, ab0:ab0 + nab], ((0, 0), (0, AB_PAD - nab)))],
        axis=1)
    pad_row = lambda v: jnp.pad(v, (0, AB_PAD - GDN_HEADS)).reshape(1, AB_PAD)
    ii = jnp.arange(TM_IN)
    tri = ((ii[:, None] >= ii[None, :]) & ((ii[:, None] // CHUNK) == (ii[None, :] // CHUNK))).astype(BF16)

    q, k, v, zs, ga, ybg, gbc, gbr = _in_call(
        x, mod3, norm1_w, w_all, gdn_conv_w[0], sc_conv_w[0], pad_row(gdn_a_log[0]), pad_row(gdn_dt_bias[0]),
        w_sc_out[0].astype(BF16), tri)

    wg = w_gdn_proj[0].astype(BF16)
    x1 = _gdn_call(q, k, v, zs, gbc, gbr, ga, ybg, x, mod3, gdn_norm_w, wg, w_o[0].astype(BF16))

    return _ffn_call(x1, mod3, modf3, norm2_w, normf_w.reshape(1, d), w_ffn_in[0].astype(BF16), w_ffn_out[0].astype(BF16))
```

```python
import jax
import jax.numpy as jnp
from jax import lax
from jax.experimental import pallas as pl
from jax.experimental.pallas import tpu as pltpu

D_MODEL = 1024
GDN_HEADS = 8
HEAD_DIM = 128
GDN_CONV = 4
SC_CONV = 3
CHUNK = 64
EPS = 1e-6
N_MOD = 6
D_FF = 2816
AB_PAD = 128

C_Q, C_K, C_V, C_Z, C_SCB, C_SCC, C_SCX, C_GA, C_GB, C_AB = (
    0, 1024, 2048, 3072, 4096, 5120, 6144, 7168, 8192, 9216)
W_ALL_COLS = C_AB + AB_PAD

TM_IN = 256
TM_GDN = 128
GDN_SEQS = 4
GDN_TILES = 16
GDN_SLAB = 512
TM_FFN = 512
FF_SLAB = 256
IN_SLAB = 256
CARRY = 8

BF16 = jnp.bfloat16
F32 = jnp.float32
MIB = 1024 * 1024


def _mm(a, b):
    return jnp.dot(a.astype(BF16), b.astype(BF16), preferred_element_type=F32)


def _bmm(a, b):
    return lax.dot_general(a.astype(BF16), b.astype(BF16), (((2,), (1,)), ((0,), (0,))),
                           preferred_element_type=F32)


def _bmm_nt(a, b):
    return lax.dot_general(a.astype(BF16), b.astype(BF16), (((2,), (2,)), ((0,), (0,))),
                           preferred_element_type=F32)


def _bmm_tn(a, b):
    return lax.dot_general(a.astype(BF16), b.astype(BF16), (((1,), (1,)), ((0,), (0,))),
                           preferred_element_type=F32)


def _silu(x):
    return x * jax.nn.sigmoid(x)


def _rms(x):
    return x * lax.rsqrt(jnp.mean(x * x, axis=-1, keepdims=True) + EPS)


def _mod_kernel(c_ref, w_ref, b_ref, o_ref):
    o_ref[...] = _mm(_silu(c_ref[...]), w_ref[...]) + b_ref[...]


def _mod_call(c, w, b):
    bsz, d = c.shape
    n = w.shape[1]
    tn = 1024
    return pl.pallas_call(
        _mod_kernel,
        grid=(n // tn,),
        in_specs=[pl.BlockSpec((bsz, d), lambda j: (0, 0)),
                  pl.BlockSpec((d, tn), lambda j: (0, j)),
                  pl.BlockSpec((1, tn), lambda j: (0, j))],
        out_specs=pl.BlockSpec((bsz, tn), lambda j: (0, j)),
        out_shape=jax.ShapeDtypeStruct((bsz, n), F32),
        compiler_params=pltpu.CompilerParams(dimension_semantics=("arbitrary",),
                                             vmem_limit_bytes=24 * MIB),
        name="mod_rows",
    )(c, w, b.reshape(1, n))


def _repack_kernel(w_ref, o_ref):
    ab0 = C_SCB
    nab = 2 * GDN_HEADS
    rows = w_ref.shape[0]
    o_ref[:, 0:ab0] = w_ref[:, 0:ab0].astype(BF16)
    o_ref[:, ab0:C_AB] = w_ref[:, ab0 + nab:C_AB + nab].astype(BF16)
    o_ref[:, C_AB:W_ALL_COLS] = jnp.concatenate(
        [w_ref[:, ab0:ab0 + nab], jnp.zeros((rows, AB_PAD - nab), F32)], axis=1).astype(BF16)


def _repack_call(w):
    d, n = w.shape
    assert n == C_AB + 2 * GDN_HEADS
    tr = 128
    return pl.pallas_call(
        _repack_kernel,
        grid=(d // tr,),
        in_specs=[pl.BlockSpec((tr, n), lambda r: (r, 0))],
        out_specs=pl.BlockSpec((tr, W_ALL_COLS), lambda r: (r, 0)),
        out_shape=jax.ShapeDtypeStruct((d, W_ALL_COLS), BF16),
        compiler_params=pltpu.CompilerParams(dimension_semantics=("arbitrary",),
                                             vmem_limit_bytes=32 * MIB),
        name="repack_w_in",
    )(w)


def _in_kernel(x_ref, mod_ref, n1_ref, w_ref, cw_ref, scw_ref, alog_ref, dtb_ref, wsc_ref, tri_ref,
               q_ref, k_ref, v_ref, zs_ref, ga_ref, ybg_ref, gbc_ref, gbr_ref, carry, pbuf, ubuf):
    t = pl.program_id(1)
    tm = TM_IN

    @pl.when(t == 0)
    def _():
        carry[...] = jnp.zeros_like(carry)

    x = x_ref[0]
    sh1 = mod_ref[0, 0:1, :]
    sc1 = mod_ref[0, 1:2, :]
    h = (_rms(x) * n1_ref[...] * (1.0 + sc1) + sh1).astype(BF16)

    sw = IN_SLAB
    heads_per_slab = sw // HEAD_DIM

    def proj(c0, width=IN_SLAB):
        return jnp.dot(h, w_ref[:, c0:c0 + width], preferred_element_type=F32)

    def causal_conv(slot, col, p, taps, width):
        pbuf[slot, 0:CARRY, :] = carry[slot, :, col:col + sw]
        pbuf[slot, CARRY:CARRY + tm, :] = p
        acc = taps[width - 1:width, :] * p
        for j in range(width - 1):
            sh = width - 1 - j
            acc = acc + taps[j:j + 1, :] * pbuf[slot, CARRY - sh:CARRY - sh + tm, :]
        carry[slot, :, col:col + sw] = p[tm - CARRY:tm]
        return acc

    def gdn_input(slot, c0, col, out_ref, norm, scale):
        taps = cw_ref[:, c0 + col:c0 + col + sw]
        y = _silu(causal_conv(slot, col, proj(c0 + col), taps, GDN_CONV))
        for i in range(heads_per_slab):
            yh = y[:, i * HEAD_DIM:(i + 1) * HEAD_DIM]
            if norm:
                yh = yh * lax.rsqrt(jnp.sum(yh * yh, axis=-1, keepdims=True) + EPS)
                if scale != 1.0:
                    yh = yh * scale
            out_ref[0, col // HEAD_DIM + i] = yh.astype(out_ref.dtype)

    def z_gate(col):
        zs = _silu(proj(C_Z + col))
        for i in range(heads_per_slab):
            zs_ref[0, col // HEAD_DIM + i] = zs[:, i * HEAD_DIM:(i + 1) * HEAD_DIM].astype(zs_ref.dtype)

    def decay_beta():
        pab = proj(C_AB, AB_PAD)
        lane = lax.broadcasted_iota(jnp.int32, (tm, AB_PAD), 1)
        xg = pab + dtb_ref[...]
        softplus = jnp.maximum(xg, 0.0) + jnp.log1p(jnp.exp(-jnp.abs(xg)))
        g = -jnp.exp(alog_ref[...]) * softplus
        gb = jnp.where(lane < GDN_HEADS, g, jax.nn.sigmoid(pab))
        g1 = gb.astype(BF16)
        r1 = gb - g1.astype(F32)
        g2 = r1.astype(BF16)
        g3 = (r1 - g2.astype(F32)).astype(BF16)
        tri = tri_ref[...]
        cum = (jnp.dot(tri, g1, preferred_element_type=F32) + jnp.dot(tri, g2, preferred_element_type=F32)
               + jnp.dot(tri, g3, preferred_element_type=F32))
        gbc = jnp.where(lane < GDN_HEADS, cum, gb)
        gbc_ref[0] = gbc
        gbr_ref[0] = jnp.transpose(gbc)[0:2 * GDN_HEADS, :]

    def short_conv_in(col):
        prod = proj(C_SCC + col) * proj(C_SCX + col)
        conv = causal_conv(3, col, prod, scw_ref[:, col:col + sw], SC_CONV)
        ubuf[:, col:col + sw] = (proj(C_SCB + col) * conv).astype(BF16)

    def short_conv_out(col):
        yb = jnp.dot(ubuf[...], wsc_ref[:, col:col + sw], preferred_element_type=F32)
        ybg_ref[0, :, col:col + sw] = (jax.nn.sigmoid(proj(C_GB + col)) * yb).astype(ybg_ref.dtype)

    def gate_a(col):
        ga_ref[0, :, col:col + sw] = jax.nn.sigmoid(proj(C_GA + col)).astype(ga_ref.dtype)

    cols = range(0, D_MODEL, sw)
    for col in cols:
        gdn_input(0, C_Q, col, q_ref, True, HEAD_DIM ** -0.5)
        short_conv_in(col)
    for col in cols:
        gdn_input(1, C_K, col, k_ref, True, 1.0)
        gate_a(col)
        z_gate(col)
    for col in cols:
        gdn_input(2, C_V, col, v_ref, False, 1.0)
        short_conv_out(col)
    decay_beta()


def _in_call(x, mod3, norm1_w, w_all, conv_w, sc_conv_w, alog_row, dtb_row, w_sc_out, tri):
    bsz, seq, d = x.shape
    tm = TM_IN
    nt = seq // tm
    const2 = lambda b, t: (0, 0)
    resident = dict(pipeline_mode=pl.Buffered(1))
    head_spec = pl.BlockSpec((1, GDN_HEADS, tm, HEAD_DIM), lambda b, t: (b, 0, t, 0))
    tok_spec = pl.BlockSpec((1, tm, d), lambda b, t: (b, t, 0))
    head_shape = jax.ShapeDtypeStruct((bsz, GDN_HEADS, seq, HEAD_DIM), BF16)
    return pl.pallas_call(
        _in_kernel,
        grid=(bsz, nt),
        in_specs=[tok_spec,
                  pl.BlockSpec((1, N_MOD, d), lambda b, t: (b, 0, 0)),
                  pl.BlockSpec((1, d), const2),
                  pl.BlockSpec((d, W_ALL_COLS), const2, **resident),
                  pl.BlockSpec((GDN_CONV, 3 * d), const2),
                  pl.BlockSpec((SC_CONV, d), const2),
                  pl.BlockSpec((1, AB_PAD), const2),
                  pl.BlockSpec((1, AB_PAD), const2),
                  pl.BlockSpec((d, d), const2, **resident),
                  pl.BlockSpec((tm, tm), const2)],
        out_specs=[head_spec, head_spec, head_spec, head_spec, tok_spec, tok_spec,
                   pl.BlockSpec((1, tm, AB_PAD), lambda b, t: (b, t, 0)),
                   pl.BlockSpec((1, 2 * GDN_HEADS, tm), lambda b, t: (b, 0, t))],
        out_shape=[head_shape, head_shape, head_shape, head_shape,
                   jax.ShapeDtypeStruct((bsz, seq, d), BF16),
                   jax.ShapeDtypeStruct((bsz, seq, d), BF16),
                   jax.ShapeDtypeStruct((bsz, seq, AB_PAD), F32),
                   jax.ShapeDtypeStruct((bsz, 2 * GDN_HEADS, seq), F32)],
        scratch_shapes=[pltpu.VMEM((4, CARRY, d), F32),
                        pltpu.VMEM((4, CARRY + tm, IN_SLAB), F32),
                        pltpu.VMEM((tm, d), BF16)],
        compiler_params=pltpu.CompilerParams(dimension_semantics=("arbitrary", "arbitrary"),
                                             vmem_limit_bytes=56 * MIB),
        name="in_proj",
    )(x, mod3, norm1_w, w_all, conv_w, sc_conv_w, alog_row, dtb_row, w_sc_out, tri)


def _block_diag(x, left):
    zero = jnp.zeros_like(x)
    return jnp.concatenate([jnp.where(left, x, zero), jnp.where(left, zero, x)], axis=1)


def _tri_inverse_stages(low, left, blk16, blk32, eye):
    def mm(x, y):
        return _bmm(x, _block_diag(y.astype(BF16), left))

    a1 = jnp.where(blk16, -low, 0.0)
    p = eye + a1
    sq = mm(a1, a1)
    yield
    for _ in range(2):
        both = mm(jnp.concatenate([p, sq], axis=1), sq)
        yield
        p = p + both[:, 0:CHUNK]
        sq = both[:, CHUNK:2 * CHUNK]
    p = p + mm(p, sq)
    yield
    for off in (jnp.where(jnp.logical_and(blk32, jnp.logical_not(blk16)), low, 0.0),
                jnp.where(blk32, 0.0, low)):
        po = mm(p, off)
        yield
        p = p - mm(po, p)
        yield
    return p


def _gdn_kernel(qn_ref, kn_ref, vn_ref, gbc_ref, gbr_ref,
                qp_ref, kp_ref, zs_ref, ga_ref, ybg_ref, x_ref, mod_ref, nw_ref, wg_ref, wo_ref,
                out_ref, state_ref, u_scr, w_scr, qk_scr, ab_scr, eb_scr, dl_scr):
    i = pl.program_id(0)
    tm = TM_GDN
    ns = GDN_SEQS
    nh = GDN_HEADS
    nb = ns * nh
    d = D_MODEL
    slot_w = lax.rem(i, 2)
    slot_r = 1 - slot_w
    t_fin = lax.rem(jnp.maximum(i - 1, 0), GDN_TILES)

    @pl.when(i == 0)
    def _():
        u_scr[1] = jnp.zeros(u_scr.shape[1:], u_scr.dtype)
        w_scr[1] = jnp.zeros(w_scr.shape[1:], w_scr.dtype)
        qk_scr[1] = jnp.zeros(qk_scr.shape[1:], qk_scr.dtype)
        ab_scr[1] = jnp.zeros(ab_scr.shape[1:], ab_scr.dtype)
        eb_scr[1] = jnp.zeros(eb_scr.shape[1:], eb_scr.dtype)
        dl_scr[1] = jnp.zeros(dl_scr.shape[1:], dl_scr.dtype)

    @pl.when(t_fin == 0)
    def _():
        state_ref[...] = jnp.zeros_like(state_ref)

    def prepare():
        gbc = gbc_ref[...]
        gbr = gbr_ref[...]

        def col_heads(j0):
            return jnp.stack([jnp.broadcast_to(gbc[sq, :, j0 + h:j0 + h + 1], (tm, HEAD_DIM))
                              for sq in range(ns) for h in range(nh)])

        def row_heads(j0):
            return jnp.stack([jnp.broadcast_to(gbr[sq, j0 + h:j0 + h + 1, :], (CHUNK, tm))
                              for sq in range(ns) for h in range(nh)])

        ri = lax.broadcasted_iota(jnp.int32, (nb, CHUNK, tm), 1)
        li = lax.broadcasted_iota(jnp.int32, (nb, CHUNK, tm), 2)
        left = li < CHUNK
        ci = jnp.where(left, li, li - CHUNK)
        causal = ri >= ci
        strict = ri > ci
        blk16 = (ri // 16) == (ci // 16)
        blk32 = (ri // 32) == (ci // 32)
        eye = jnp.where(ri == ci, 1.0, 0.0).astype(F32)

        def side_by_side(g):
            return jnp.where(left, g[:, 0:CHUNK, :], g[:, CHUNK:tm, :])

        q = qn_ref[...].reshape(nb, tm, HEAD_DIM)
        k = kn_ref[...].reshape(nb, tm, HEAD_DIM)
        v = vn_ref[...].reshape(nb, tm, HEAD_DIM)

        gkk = side_by_side(_bmm_nt(k, k))
        yield
        gqk = side_by_side(_bmm_nt(q, k))
        yield

        gcb = col_heads(0)
        ab = jnp.exp(gcb)
        gc2 = gcb.reshape(nb * 2, CHUNK, HEAD_DIM)
        ab_scr[slot_w] = ab.astype(ab_scr.dtype)
        eb_scr[slot_w] = jnp.exp(gc2[:, CHUNK - 1:CHUNK, :] - gc2).reshape(nb, tm, HEAD_DIM).astype(eb_scr.dtype)
        dl_scr[slot_w] = ab.reshape(nb, 2, CHUNK, HEAD_DIM)[:, :, CHUNK - 1, :]
        grow = row_heads(0)
        brow = row_heads(nh)
        barow = brow * jnp.exp(grow)

        dec = jnp.where(causal, jnp.exp(jnp.minimum(side_by_side(gcb) - grow, 0.0)), 0.0)
        bcol = side_by_side(col_heads(nh))
        low = jnp.where(strict, gkk * bcol * dec, 0.0)
        qk = jnp.where(causal, gqk * dec, 0.0).astype(qk_scr.dtype)
        qk_scr[slot_w, 0] = qk[:, :, 0:CHUNK]
        qk_scr[slot_w, 1] = qk[:, :, CHUNK:tm]
        tinv = yield from _tri_inverse_stages(low, left, blk16, blk32, eye)
        u_scr[slot_w] = _bmm(_block_diag((tinv * brow).astype(BF16), left), v)
        yield
        w_scr[slot_w] = _bmm(_block_diag((tinv * barow).astype(BF16), left), k).astype(w_scr.dtype)
        yield

    def finish():
        q = qp_ref[...].reshape(nb, tm, HEAD_DIM)
        k = kp_ref[...].reshape(nb, tm, HEAD_DIM)
        s = state_ref[...]
        outs = []
        for c in range(tm // CHUNK):
            rows = slice(c * CHUNK, (c + 1) * CHUNK)
            ws_qs = _bmm(jnp.concatenate([w_scr[slot_r, :, rows, :], q[:, rows]], axis=1), s)
            yield
            v_new = u_scr[slot_r, :, rows, :] - ws_qs[:, 0:CHUNK]
            outs.append(ab_scr[slot_r, :, rows, :].astype(F32) * ws_qs[:, CHUNK:]
                        + _bmm(qk_scr[slot_r, c], v_new))
            yield
            s = (s * dl_scr[slot_r, :, c:c + 1, :]
                 + _bmm_tn(k[:, rows], eb_scr[slot_r, :, rows, :].astype(F32) * v_new))
            yield
        state_ref[...] = s

        o = jnp.concatenate(outs, axis=1)
        on = (_rms(o) * nw_ref[...] * zs_ref[...].reshape(nb, tm, HEAD_DIM).astype(F32)).astype(BF16)
        o2d = jnp.concatenate([jnp.concatenate([on[sq * nh + h] for h in range(nh)], axis=1)
                               for sq in range(ns)], axis=0)
        ga = ga_ref[...].reshape(ns * tm, d)
        ybg = ybg_ref[...].reshape(ns * tm, d)
        merged = []
        for c0 in range(0, d, GDN_SLAB):
            ya = jnp.dot(o2d, wg_ref[:, c0:c0 + GDN_SLAB], preferred_element_type=F32)
            merged.append((ga[:, c0:c0 + GDN_SLAB].astype(F32) * ya
                           + ybg[:, c0:c0 + GDN_SLAB].astype(F32)).astype(BF16))
            yield
        merged = jnp.concatenate(merged, axis=1)
        for c0 in range(0, d, GDN_SLAB):
            mix = jnp.dot(merged, wo_ref[:, c0:c0 + GDN_SLAB], preferred_element_type=F32)
            for sq in range(ns):
                out_ref[sq, :, c0:c0 + GDN_SLAB] = (x_ref[sq, :, c0:c0 + GDN_SLAB]
                                                    + mod_ref[sq, 2:3, c0:c0 + GDN_SLAB] * mix[sq * tm:(sq + 1) * tm])
            yield

    active = [finish(), prepare()]
    while active:
        for stage in list(active):
            try:
                next(stage)
            except StopIteration:
                active.remove(stage)


def _gdn_call(q, k, v, zs, gbc, gbr, ga, ybg, x, mod3, gdn_norm_w, wg, wo):
    bsz, seq, d = x.shape
    tm = TM_GDN
    ns = GDN_SEQS
    nb = ns * GDN_HEADS
    assert seq // tm == GDN_TILES and tm == 2 * CHUNK and bsz % ns == 0
    last = (bsz // ns) * GDN_TILES - 1

    def tile(j):
        return j // GDN_TILES, lax.rem(j, GDN_TILES)

    def nxt(i):
        return tile(jnp.minimum(i, last))

    def fin(i):
        return tile(jnp.maximum(i - 1, 0))

    const2 = lambda i: (0, 0)
    resident = dict(pipeline_mode=pl.Buffered(1))
    head_blk = (ns, GDN_HEADS, tm, HEAD_DIM)
    head_nxt = pl.BlockSpec(head_blk, lambda i: (nxt(i)[0], 0, nxt(i)[1], 0))
    head_fin = pl.BlockSpec(head_blk, lambda i: (fin(i)[0], 0, fin(i)[1], 0))
    tok_fin = pl.BlockSpec((ns, tm, d), lambda i: (fin(i)[0], fin(i)[1], 0))
    return pl.pallas_call(
        _gdn_kernel,
        grid=(last + 2,),
        in_specs=[head_nxt, head_nxt, head_nxt,
                  pl.BlockSpec((ns, tm, AB_PAD), lambda i: (nxt(i)[0], nxt(i)[1], 0)),
                  pl.BlockSpec((ns, 2 * GDN_HEADS, tm), lambda i: (nxt(i)[0], 0, nxt(i)[1])),
                  head_fin, head_fin, head_fin, tok_fin, tok_fin, tok_fin,
                  pl.BlockSpec((ns, N_MOD, d), lambda i: (fin(i)[0], 0, 0)),
                  pl.BlockSpec((1, HEAD_DIM), const2),
                  pl.BlockSpec((d, d), const2, **resident),
                  pl.BlockSpec((d, d), const2, **resident)],
        out_specs=tok_fin,
        out_shape=jax.ShapeDtypeStruct((bsz, seq, d), F32),
        scratch_shapes=[pltpu.VMEM((nb, HEAD_DIM, HEAD_DIM), F32),
                        pltpu.VMEM((2, nb, tm, HEAD_DIM), F32),
                        pltpu.VMEM((2, nb, tm, HEAD_DIM), BF16),
                        pltpu.VMEM((2, 2, nb, CHUNK, CHUNK), BF16),
                        pltpu.VMEM((2, nb, tm, HEAD_DIM), BF16),
                        pltpu.VMEM((2, nb, tm, HEAD_DIM), BF16),
                        pltpu.VMEM((2, nb, 2, HEAD_DIM), F32)],
        compiler_params=pltpu.CompilerParams(dimension_semantics=("arbitrary",),
                                             vmem_limit_bytes=56 * MIB),
        name="gdn_mix",
    )(q, k, v, gbc, gbr, q, k, zs, ga, ybg, x, mod3, gdn_norm_w, wg, wo)


def _ffn_kernel(x_ref, mod_ref, modf_ref, n2_ref, nf_ref, win_ref, wout_ref, out_ref):
    x = x_ref[0]
    sh2 = mod_ref[0, 3:4, :]
    sc2 = mod_ref[0, 4:5, :]
    g2 = mod_ref[0, 5:6, :]
    h = (_rms(x) * n2_ref[...] * (1.0 + sc2) + sh2).astype(BF16)
    acc = jnp.zeros(x.shape, F32)
    for s0 in range(0, D_FF, FF_SLAB):
        gate = jnp.dot(h, win_ref[:, s0:s0 + FF_SLAB], preferred_element_type=F32)
        up = jnp.dot(h, win_ref[:, D_FF + s0:D_FF + s0 + FF_SLAB], preferred_element_type=F32)
        act = (_silu(gate) * up).astype(BF16)
        acc = acc + jnp.dot(act, wout_ref[s0:s0 + FF_SLAB, :], preferred_element_type=F32)
    x2 = x + g2 * acc
    out_ref[0] = _rms(x2) * nf_ref[...] * (1.0 + modf_ref[0, 1:2, :]) + modf_ref[0, 0:1, :]


def _ffn_call(x, mod3, modf3, norm2_w, normf_w, w_in, w_out):
    bsz, seq, d = x.shape
    tm = TM_FFN
    const2 = lambda b, t: (0, 0)
    resident = dict(pipeline_mode=pl.Buffered(1))
    tok_spec = pl.BlockSpec((1, tm, d), lambda b, t: (b, t, 0))
    return pl.pallas_call(
        _ffn_kernel,
        grid=(bsz, seq // tm),
        in_specs=[tok_spec,
                  pl.BlockSpec((1, N_MOD, d), lambda b, t: (b, 0, 0)),
                  pl.BlockSpec((1, 2, d), lambda b, t: (b, 0, 0)),
                  pl.BlockSpec((1, d), const2),
                  pl.BlockSpec((1, d), const2),
                  pl.BlockSpec((d, 2 * D_FF), const2, **resident),
                  pl.BlockSpec((D_FF, d), const2, **resident)],
        out_specs=tok_spec,
        out_shape=jax.ShapeDtypeStruct((bsz, seq, d), F32),
        compiler_params=pltpu.CompilerParams(dimension_semantics=("arbitrary", "arbitrary"),
                                             vmem_limit_bytes=48 * MIB),
        name="ffn_final",
    )(x, mod3, modf3, norm2_w, normf_w, w_in, w_out)


def kernel(x, c, w_ada, b_ada, norm1_w, w_in, gdn_conv_w, gdn_a_log, gdn_dt_bias, gdn_norm_w, w_gdn_proj,
           sc_conv_w, w_sc_out, w_o, norm2_w, w_ffn_in, w_ffn_out, w_ada_f, b_ada_f, normf_w):
    bsz, seq, d = x.shape
    assert w_ada.shape[0] == 1 and d == D_MODEL and seq % TM_FFN == 0

    mod3 = _mod_call(c, w_ada[0], b_ada[0]).reshape(bsz, N_MOD, d)
    modf3 = _mod_call(c, w_ada_f, b_ada_f).reshape(bsz, 2, d)

    w_all = _repack_call(w_in[0])
    pad_row = lambda v: jnp.pad(v, (0, AB_PAD - GDN_HEADS)).reshape(1, AB_PAD)
    ii = jnp.arange(TM_IN)
    tri = ((ii[:, None] >= ii[None, :]) & ((ii[:, None] // CHUNK) == (ii[None, :] // CHUNK))).astype(BF16)

    q, k, v, zs, ga, ybg, gbc, gbr = _in_call(
        x, mod3, norm1_w, w_all, gdn_conv_w[0], sc_conv_w[0], pad_row(gdn_a_log[0]), pad_row(gdn_dt_bias[0]),
        w_sc_out[0].astype(BF16), tri)

    wg = w_gdn_proj[0].astype(BF16)
    x1 = _gdn_call(q, k, v, zs, gbc, gbr, ga, ybg, x, mod3, gdn_norm_w, wg, w_o[0].astype(BF16))

    return _ffn_call(x1, mod3, modf3, norm2_w, normf_w.reshape(1, d), w_ffn_in[0].astype(BF16), w_ffn_out[0].astype(BF16))
```

```python
import jax
import jax.numpy as jnp
from jax import lax
from jax.experimental import pallas as pl
from jax.experimental.pallas import tpu as pltpu

D_MODEL = 1024
GDN_HEADS = 8
HEAD_DIM = 128
GDN_CONV = 4
SC_CONV = 3
CHUNK = 64
EPS = 1e-6
N_MOD = 6
D_FF = 2816
AB_PAD = 128

C_Q, C_K, C_V, C_Z, C_SCB, C_SCC, C_SCX, C_GA, C_GB, C_AB = (
    0, 1024, 2048, 3072, 4096, 5120, 6144, 7168, 8192, 9216)
W_ALL_COLS = C_AB + AB_PAD

TM_IN = 256
TM_GDN = 128
GDN_SEQS = 4
GDN_TILES = 16
GDN_SLAB = 512
TM_FFN = 512
FF_SLAB = 256
IN_SLAB = 256
CARRY = 8
REPACK_SLAB = 256

BF16 = jnp.bfloat16
F32 = jnp.float32
MIB = 1024 * 1024


def _mm(a, b):
    return jnp.dot(a.astype(BF16), b.astype(BF16), preferred_element_type=F32)


def _bmm(a, b):
    return lax.dot_general(a.astype(BF16), b.astype(BF16), (((2,), (1,)), ((0,), (0,))),
                           preferred_element_type=F32)


def _bmm_nt(a, b):
    return lax.dot_general(a.astype(BF16), b.astype(BF16), (((2,), (2,)), ((0,), (0,))),
                           preferred_element_type=F32)


def _bmm_tn(a, b):
    return lax.dot_general(a.astype(BF16), b.astype(BF16), (((1,), (1,)), ((0,), (0,))),
                           preferred_element_type=F32)


def _silu(x):
    return x * jax.nn.sigmoid(x)


def _rms(x):
    return x * lax.rsqrt(jnp.mean(x * x, axis=-1, keepdims=True) + EPS)


def _mod_kernel(c_ref, w_ref, b_ref, o_ref):
    o_ref[...] = _mm(_silu(c_ref[...]), w_ref[...]) + b_ref[...]


def _mod_call(c, w, b):
    bsz, d = c.shape
    n = w.shape[1]
    tn = 1024
    return pl.pallas_call(
        _mod_kernel,
        grid=(n // tn,),
        in_specs=[pl.BlockSpec((bsz, d), lambda j: (0, 0)),
                  pl.BlockSpec((d, tn), lambda j: (0, j)),
                  pl.BlockSpec((1, tn), lambda j: (0, j))],
        out_specs=pl.BlockSpec((bsz, tn), lambda j: (0, j)),
        out_shape=jax.ShapeDtypeStruct((bsz, n), F32),
        compiler_params=pltpu.CompilerParams(dimension_semantics=("arbitrary",),
                                             vmem_limit_bytes=24 * MIB),
        name="mod_rows",
    )(c, w, b.reshape(1, n))


def _repack_kernel(wt_ref, ab_ref, o_ref):
    j = pl.program_id(0)

    @pl.when(j < C_AB // REPACK_SLAB)
    def _():
        o_ref[...] = jnp.transpose(wt_ref[...]).astype(BF16)

    @pl.when(j == C_AB // REPACK_SLAB)
    def _():
        o_ref[...] = jnp.zeros(o_ref.shape, BF16)
        o_ref[:, 0:2 * GDN_HEADS] = jnp.transpose(ab_ref[...]).astype(BF16)


def _repack_call(wt):
    n, d = wt.shape
    nab = 2 * GDN_HEADS
    assert n == C_AB + nab
    sl = REPACK_SLAB
    n_main = C_AB // sl
    n_head = C_SCB // sl

    def src_row(j):
        jj = jnp.minimum(j, n_main - 1)
        return (jj * (sl // nab) + jnp.where(jj < n_head, 0, 1)) * nab

    return pl.pallas_call(
        _repack_kernel,
        grid=(n_main + 1,),
        in_specs=[pl.BlockSpec((pl.Element(sl), pl.Element(d)), lambda j: (src_row(j), 0)),
                  pl.BlockSpec((pl.Element(nab), pl.Element(d)), lambda j: (C_SCB, 0))],
        out_specs=pl.BlockSpec((d, sl), lambda j: (0, j)),
        out_shape=jax.ShapeDtypeStruct((d, W_ALL_COLS), BF16),
        compiler_params=pltpu.CompilerParams(dimension_semantics=("arbitrary",),
                                             vmem_limit_bytes=32 * MIB),
        name="repack_w_in",
    )(wt, wt)


def _in_kernel(x_ref, mod_ref, n1_ref, w_ref, cw_ref, scw_ref, alog_ref, dtb_ref, wsc_ref, tri_ref,
               q_ref, k_ref, v_ref, zs_ref, ga_ref, ybg_ref, gbc_ref, gbr_ref, carry, pbuf, ubuf):
    t = pl.program_id(1)
    tm = TM_IN

    @pl.when(t == 0)
    def _():
        carry[...] = jnp.zeros_like(carry)

    x = x_ref[0]
    sh1 = mod_ref[0, 0:1, :]
    sc1 = mod_ref[0, 1:2, :]
    h = (_rms(x) * n1_ref[...] * (1.0 + sc1) + sh1).astype(BF16)

    sw = IN_SLAB
    heads_per_slab = sw // HEAD_DIM

    def proj(c0, width=IN_SLAB):
        return jnp.dot(h, w_ref[:, c0:c0 + width], preferred_element_type=F32)

    def causal_conv(slot, col, p, taps, width):
        pbuf[slot, 0:CARRY, :] = carry[slot, :, col:col + sw]
        pbuf[slot, CARRY:CARRY + tm, :] = p
        acc = taps[width - 1:width, :] * p
        for j in range(width - 1):
            sh = width - 1 - j
            acc = acc + taps[j:j + 1, :] * pbuf[slot, CARRY - sh:CARRY - sh + tm, :]
        carry[slot, :, col:col + sw] = p[tm - CARRY:tm]
        return acc

    def gdn_input(slot, c0, col, out_ref, norm, scale):
        taps = cw_ref[:, c0 + col:c0 + col + sw]
        y = _silu(causal_conv(slot, col, proj(c0 + col), taps, GDN_CONV))
        for i in range(heads_per_slab):
            yh = y[:, i * HEAD_DIM:(i + 1) * HEAD_DIM]
            if norm:
                yh = yh * lax.rsqrt(jnp.sum(yh * yh, axis=-1, keepdims=True) + EPS)
                if scale != 1.0:
                    yh = yh * scale
            out_ref[0, col // HEAD_DIM + i] = yh.astype(out_ref.dtype)

    def z_gate(col):
        zs = _silu(proj(C_Z + col))
        for i in range(heads_per_slab):
            zs_ref[0, col // HEAD_DIM + i] = zs[:, i * HEAD_DIM:(i + 1) * HEAD_DIM].astype(zs_ref.dtype)

    def decay_beta():
        pab = proj(C_AB, AB_PAD)
        lane = lax.broadcasted_iota(jnp.int32, (tm, AB_PAD), 1)
        xg = pab + dtb_ref[...]
        softplus = jnp.maximum(xg, 0.0) + jnp.log1p(jnp.exp(-jnp.abs(xg)))
        g = -jnp.exp(alog_ref[...]) * softplus
        gb = jnp.where(lane < GDN_HEADS, g, jax.nn.sigmoid(pab))
        g1 = gb.astype(BF16)
        r1 = gb - g1.astype(F32)
        g2 = r1.astype(BF16)
        g3 = (r1 - g2.astype(F32)).astype(BF16)
        tri = tri_ref[...]
        cum = (jnp.dot(tri, g1, preferred_element_type=F32) + jnp.dot(tri, g2, preferred_element_type=F32)
               + jnp.dot(tri, g3, preferred_element_type=F32))
        gbc = jnp.where(lane < GDN_HEADS, cum, gb)
        gbc_ref[0] = gbc
        gbr_ref[0] = jnp.transpose(gbc)[0:2 * GDN_HEADS, :]

    def short_conv_in(col):
        prod = proj(C_SCC + col) * proj(C_SCX + col)
        conv = causal_conv(3, col, prod, scw_ref[:, col:col + sw], SC_CONV)
        ubuf[:, col:col + sw] = (proj(C_SCB + col) * conv).astype(BF16)

    def short_conv_out(col):
        yb = jnp.dot(ubuf[...], wsc_ref[:, col:col + sw], preferred_element_type=F32)
        ybg_ref[0, :, col:col + sw] = (jax.nn.sigmoid(proj(C_GB + col)) * yb).astype(ybg_ref.dtype)

    def gate_a(col):
        ga_ref[0, :, col:col + sw] = jax.nn.sigmoid(proj(C_GA + col)).astype(ga_ref.dtype)

    cols = range(0, D_MODEL, sw)
    for col in cols:
        gdn_input(0, C_Q, col, q_ref, True, HEAD_DIM ** -0.5)
        short_conv_in(col)
    for col in cols:
        gdn_input(1, C_K, col, k_ref, True, 1.0)
        gate_a(col)
        z_gate(col)
    for col in cols:
        gdn_input(2, C_V, col, v_ref, False, 1.0)
        short_conv_out(col)
    decay_beta()


def _in_call(x, mod3, norm1_w, w_all, conv_w, sc_conv_w, alog_row, dtb_row, w_sc_out, tri):
    bsz, seq, d = x.shape
    tm = TM_IN
    nt = seq // tm
    const2 = lambda b, t: (0, 0)
    resident = dict(pipeline_mode=pl.Buffered(1))
    head_spec = pl.BlockSpec((1, GDN_HEADS, tm, HEAD_DIM), lambda b, t: (b, 0, t, 0))
    tok_spec = pl.BlockSpec((1, tm, d), lambda b, t: (b, t, 0))
    head_shape = jax.ShapeDtypeStruct((bsz, GDN_HEADS, seq, HEAD_DIM), BF16)
    return pl.pallas_call(
        _in_kernel,
        grid=(bsz, nt),
        in_specs=[tok_spec,
                  pl.BlockSpec((1, N_MOD, d), lambda b, t: (b, 0, 0)),
                  pl.BlockSpec((1, d), const2),
                  pl.BlockSpec((d, W_ALL_COLS), const2, **resident),
                  pl.BlockSpec((GDN_CONV, 3 * d), const2),
                  pl.BlockSpec((SC_CONV, d), const2),
                  pl.BlockSpec((1, AB_PAD), const2),
                  pl.BlockSpec((1, AB_PAD), const2),
                  pl.BlockSpec((d, d), const2, **resident),
                  pl.BlockSpec((tm, tm), const2)],
        out_specs=[head_spec, head_spec, head_spec, head_spec, tok_spec, tok_spec,
                   pl.BlockSpec((1, tm, AB_PAD), lambda b, t: (b, t, 0)),
                   pl.BlockSpec((1, 2 * GDN_HEADS, tm), lambda b, t: (b, 0, t))],
        out_shape=[head_shape, head_shape, head_shape, head_shape,
                   jax.ShapeDtypeStruct((bsz, seq, d), BF16),
                   jax.ShapeDtypeStruct((bsz, seq, d), BF16),
                   jax.ShapeDtypeStruct((bsz, seq, AB_PAD), F32),
                   jax.ShapeDtypeStruct((bsz, 2 * GDN_HEADS, seq), F32)],
        scratch_shapes=[pltpu.VMEM((4, CARRY, d), F32),
                        pltpu.VMEM((4, CARRY + tm, IN_SLAB), F32),
                        pltpu.VMEM((tm, d), BF16)],
        compiler_params=pltpu.CompilerParams(dimension_semantics=("arbitrary", "arbitrary"),
                                             vmem_limit_bytes=56 * MIB),
        name="in_proj",
    )(x, mod3, norm1_w, w_all, conv_w, sc_conv_w, alog_row, dtb_row, w_sc_out, tri)


def _block_diag(x, left):
    zero = jnp.zeros_like(x)
    return jnp.concatenate([jnp.where(left, x, zero), jnp.where(left, zero, x)], axis=1)


def _tri_inverse_stages(low, left, blk16, blk32, eye):
    def mm(x, y):
        return _bmm(x, _block_diag(y.astype(BF16), left))

    a1 = jnp.where(blk16, -low, 0.0)
    p = eye + a1
    sq = mm(a1, a1)
    yield
    for _ in range(2):
        both = mm(jnp.concatenate([p, sq], axis=1), sq)
        yield
        p = p + both[:, 0:CHUNK]
        sq = both[:, CHUNK:2 * CHUNK]
    p = p + mm(p, sq)
    yield
    for off in (jnp.where(jnp.logical_and(blk32, jnp.logical_not(blk16)), low, 0.0),
                jnp.where(blk32, 0.0, low)):
        po = mm(p, off)
        yield
        p = p - mm(po, p)
        yield
    return p


def _gdn_kernel(qn_ref, kn_ref, vn_ref, gbc_ref, gbr_ref,
                qp_ref, kp_ref, zs_ref, ga_ref, ybg_ref, x_ref, mod_ref, nw_ref, wg_ref, wo_ref,
                out_ref, state_ref, u_scr, w_scr, qk_scr, ab_scr, eb_scr, dl_scr):
    i = pl.program_id(0)
    tm = TM_GDN
    ns = GDN_SEQS
    nh = GDN_HEADS
    nb = ns * nh
    d = D_MODEL
    slot_w = lax.rem(i, 2)
    slot_r = 1 - slot_w
    t_fin = lax.rem(jnp.maximum(i - 1, 0), GDN_TILES)

    @pl.when(i == 0)
    def _():
        u_scr[1] = jnp.zeros(u_scr.shape[1:], u_scr.dtype)
        w_scr[1] = jnp.zeros(w_scr.shape[1:], w_scr.dtype)
        qk_scr[1] = jnp.zeros(qk_scr.shape[1:], qk_scr.dtype)
        ab_scr[1] = jnp.zeros(ab_scr.shape[1:], ab_scr.dtype)
        eb_scr[1] = jnp.zeros(eb_scr.shape[1:], eb_scr.dtype)
        dl_scr[1] = jnp.zeros(dl_scr.shape[1:], dl_scr.dtype)

    @pl.when(t_fin == 0)
    def _():
        state_ref[...] = jnp.zeros_like(state_ref)

    def prepare():
        gbc = gbc_ref[...]
        gbr = gbr_ref[...]

        def col_heads(j0):
            return jnp.stack([jnp.broadcast_to(gbc[sq, :, j0 + h:j0 + h + 1], (tm, HEAD_DIM))
                              for sq in range(ns) for h in range(nh)])

        def row_heads(j0):
            return jnp.stack([jnp.broadcast_to(gbr[sq, j0 + h:j0 + h + 1, :], (CHUNK, tm))
                              for sq in range(ns) for h in range(nh)])

        ri = lax.broadcasted_iota(jnp.int32, (nb, CHUNK, tm), 1)
        li = lax.broadcasted_iota(jnp.int32, (nb, CHUNK, tm), 2)
        left = li < CHUNK
        ci = jnp.where(left, li, li - CHUNK)
        causal = ri >= ci
        strict = ri > ci
        blk16 = (ri // 16) == (ci // 16)
        blk32 = (ri // 32) == (ci // 32)
        eye = jnp.where(ri == ci, 1.0, 0.0).astype(F32)

        def side_by_side(g):
            return jnp.where(left, g[:, 0:CHUNK, :], g[:, CHUNK:tm, :])

        q = qn_ref[...].reshape(nb, tm, HEAD_DIM)
        k = kn_ref[...].reshape(nb, tm, HEAD_DIM)
        v = vn_ref[...].reshape(nb, tm, HEAD_DIM)

        gkk = side_by_side(_bmm_nt(k, k))
        yield
        gqk = side_by_side(_bmm_nt(q, k))
        yield

        gcb = col_heads(0)
        ab = jnp.exp(gcb)
        gc2 = gcb.reshape(nb * 2, CHUNK, HEAD_DIM)
        ab_scr[slot_w] = ab.astype(ab_scr.dtype)
        eb_scr[slot_w] = jnp.exp(gc2[:, CHUNK - 1:CHUNK, :] - gc2).reshape(nb, tm, HEAD_DIM).astype(eb_scr.dtype)
        dl_scr[slot_w] = ab.reshape(nb, 2, CHUNK, HEAD_DIM)[:, :, CHUNK - 1, :]
        grow = row_heads(0)
        brow = row_heads(nh)
        barow = brow * jnp.exp(grow)

        dec = jnp.where(causal, jnp.exp(jnp.minimum(side_by_side(gcb) - grow, 0.0)), 0.0)
        bcol = side_by_side(col_heads(nh))
        low = jnp.where(strict, gkk * bcol * dec, 0.0)
        qk = jnp.where(causal, gqk * dec, 0.0).astype(qk_scr.dtype)
        qk_scr[slot_w, 0] = qk[:, :, 0:CHUNK]
        qk_scr[slot_w, 1] = qk[:, :, CHUNK:tm]
        tinv = yield from _tri_inverse_stages(low, left, blk16, blk32, eye)
        u_scr[slot_w] = _bmm(_block_diag((tinv * brow).astype(BF16), left), v)
        yield
        w_scr[slot_w] = _bmm(_block_diag((tinv * barow).astype(BF16), left), k).astype(w_scr.dtype)
        yield

    def finish():
        q = qp_ref[...].reshape(nb, tm, HEAD_DIM)
        k = kp_ref[...].reshape(nb, tm, HEAD_DIM)
        s = state_ref[...]
        outs = []
        for c in range(tm // CHUNK):
            rows = slice(c * CHUNK, (c + 1) * CHUNK)
            ws_qs = _bmm(jnp.concatenate([w_scr[slot_r, :, rows, :], q[:, rows]], axis=1), s)
            yield
            v_new = u_scr[slot_r, :, rows, :] - ws_qs[:, 0:CHUNK]
            outs.append(ab_scr[slot_r, :, rows, :].astype(F32) * ws_qs[:, CHUNK:]
                        + _bmm(qk_scr[slot_r, c], v_new))
            yield
            s = (s * dl_scr[slot_r, :, c:c + 1, :]
                 + _bmm_tn(k[:, rows], eb_scr[slot_r, :, rows, :].astype(F32) * v_new))
            yield
        state_ref[...] = s

        o = jnp.concatenate(outs, axis=1)
        on = (_rms(o) * nw_ref[...] * zs_ref[...].reshape(nb, tm, HEAD_DIM).astype(F32)).astype(BF16)
        o2d = jnp.concatenate([jnp.concatenate([on[sq * nh + h] for h in range(nh)], axis=1)
                               for sq in range(ns)], axis=0)
        ga = ga_ref[...].reshape(ns * tm, d)
        ybg = ybg_ref[...].reshape(ns * tm, d)
        merged = []
        for c0 in range(0, d, GDN_SLAB):
            ya = jnp.dot(o2d, wg_ref[:, c0:c0 + GDN_SLAB], preferred_element_type=F32)
            merged.append((ga[:, c0:c0 + GDN_SLAB].astype(F32) * ya
                           + ybg[:, c0:c0 + GDN_SLAB].astype(F32)).astype(BF16))
            yield
        merged = jnp.concatenate(merged, axis=1)
        for c0 in range(0, d, GDN_SLAB):
            mix = jnp.dot(merged, wo_ref[:, c0:c0 + GDN_SLAB], preferred_element_type=F32)
            for sq in range(ns):
                out_ref[sq, :, c0:c0 + GDN_SLAB] = (x_ref[sq, :, c0:c0 + GDN_SLAB]
                                                    + mod_ref[sq, 2:3, c0:c0 + GDN_SLAB] * mix[sq * tm:(sq + 1) * tm])
            yield

    active = [finish(), prepare()]
    while active:
        for stage in list(active):
            try:
                next(stage)
            except StopIteration:
                active.remove(stage)


def _gdn_call(q, k, v, zs, gbc, gbr, ga, ybg, x, mod3, gdn_norm_w, wg, wo):
    bsz, seq, d = x.shape
    tm = TM_GDN
    ns = GDN_SEQS
    nb = ns * GDN_HEADS
    assert seq // tm == GDN_TILES and tm == 2 * CHUNK and bsz % ns == 0
    last = (bsz // ns) * GDN_TILES - 1

    def tile(j):
        return j // GDN_TILES, lax.rem(j, GDN_TILES)

    def nxt(i):
        return tile(jnp.minimum(i, last))

    def fin(i):
        return tile(jnp.maximum(i - 1, 0))

    const2 = lambda i: (0, 0)
    resident = dict(pipeline_mode=pl.Buffered(1))
    head_blk = (ns, GDN_HEADS, tm, HEAD_DIM)
    head_nxt = pl.BlockSpec(head_blk, lambda i: (nxt(i)[0], 0, nxt(i)[1], 0))
    head_fin = pl.BlockSpec(head_blk, lambda i: (fin(i)[0], 0, fin(i)[1], 0))
    tok_fin = pl.BlockSpec((ns, tm, d), lambda i: (fin(i)[0], fin(i)[1], 0))
    return pl.pallas_call(
        _gdn_kernel,
        grid=(last + 2,),
        in_specs=[head_nxt, head_nxt, head_nxt,
                  pl.BlockSpec((ns, tm, AB_PAD), lambda i: (nxt(i)[0], nxt(i)[1], 0)),
                  pl.BlockSpec((ns, 2 * GDN_HEADS, tm), lambda i: (nxt(i)[0], 0, nxt(i)[1])),
                  head_fin, head_fin, head_fin, tok_fin, tok_fin, tok_fin,
                  pl.BlockSpec((ns, N_MOD, d), lambda i: (fin(i)[0], 0, 0)),
                  pl.BlockSpec((1, HEAD_DIM), const2),
                  pl.BlockSpec((d, d), const2, **resident),
                  pl.BlockSpec((d, d), const2, **resident)],
        out_specs=tok_fin,
        out_shape=jax.ShapeDtypeStruct((bsz, seq, d), F32),
        scratch_shapes=[pltpu.VMEM((nb, HEAD_DIM, HEAD_DIM), F32),
                        pltpu.VMEM((2, nb, tm, HEAD_DIM), F32),
                        pltpu.VMEM((2, nb, tm, HEAD_DIM), BF16),
                        pltpu.VMEM((2, 2, nb, CHUNK, CHUNK), BF16),
                        pltpu.VMEM((2, nb, tm, HEAD_DIM), BF16),
                        pltpu.VMEM((2, nb, tm, HEAD_DIM), BF16),
                        pltpu.VMEM((2, nb, 2, HEAD_DIM), F32)],
        compiler_params=pltpu.CompilerParams(dimension_semantics=("arbitrary",),
                                             vmem_limit_bytes=56 * MIB),
        name="gdn_mix",
    )(q, k, v, gbc, gbr, q, k, zs, ga, ybg, x, mod3, gdn_norm_w, wg, wo)


def _ffn_kernel(x_ref, mod_ref, modf_ref, n2_ref, nf_ref, win_ref, wout_ref, out_ref):
    x = x_ref[0]
    sh2 = mod_ref[0, 3:4, :]
    sc2 = mod_ref[0, 4:5, :]
    g2 = mod_ref[0, 5:6, :]
    h = (_rms(x) * n2_ref[...] * (1.0 + sc2) + sh2).astype(BF16)
    acc = jnp.zeros(x.shape, F32)
    for s0 in range(0, D_FF, FF_SLAB):
        gate = jnp.dot(h, win_ref[:, s0:s0 + FF_SLAB], preferred_element_type=F32)
        up = jnp.dot(h, win_ref[:, D_FF + s0:D_FF + s0 + FF_SLAB], preferred_element_type=F32)
        act = (_silu(gate) * up).astype(BF16)
        acc = acc + jnp.dot(act, wout_ref[s0:s0 + FF_SLAB, :], preferred_element_type=F32)
    x2 = x + g2 * acc
    out_ref[0] = _rms(x2) * nf_ref[...] * (1.0 + modf_ref[0, 1:2, :]) + modf_ref[0, 0:1, :]


def _ffn_call(x, mod3, modf3, norm2_w, normf_w, w_in, w_out):
    bsz, seq, d = x.shape
    tm = TM_FFN
    const2 = lambda b, t: (0, 0)
    resident = dict(pipeline_mode=pl.Buffered(1))
    tok_spec = pl.BlockSpec((1, tm, d), lambda b, t: (b, t, 0))
    return pl.pallas_call(
        _ffn_kernel,
        grid=(bsz, seq // tm),
        in_specs=[tok_spec,
                  pl.BlockSpec((1, N_MOD, d), lambda b, t: (b, 0, 0)),
                  pl.BlockSpec((1, 2, d), lambda b, t: (b, 0, 0)),
                  pl.BlockSpec((1, d), const2),
                  pl.BlockSpec((1, d), const2),
                  pl.BlockSpec((d, 2 * D_FF), const2, **resident),
                  pl.BlockSpec((D_FF, d), const2, **resident)],
        out_specs=tok_spec,
        out_shape=jax.ShapeDtypeStruct((bsz, seq, d), F32),
        compiler_params=pltpu.CompilerParams(dimension_semantics=("arbitrary", "arbitrary"),
                                             vmem_limit_bytes=48 * MIB),
        name="ffn_final",
    )(x, mod3, modf3, norm2_w, normf_w, w_in, w_out)


def kernel(x, c, w_ada, b_ada, norm1_w, w_in, gdn_conv_w, gdn_a_log, gdn_dt_bias, gdn_norm_w, w_gdn_proj,
           sc_conv_w, w_sc_out, w_o, norm2_w, w_ffn_in, w_ffn_out, w_ada_f, b_ada_f, normf_w):
    bsz, seq, d = x.shape
    assert w_ada.shape[0] == 1 and d == D_MODEL and seq % TM_FFN == 0

    mod3 = _mod_call(c, w_ada[0], b_ada[0]).reshape(bsz, N_MOD, d)
    modf3 = _mod_call(c, w_ada_f, b_ada_f).reshape(bsz, 2, d)

    w_all = _repack_call(jnp.transpose(w_in[0]))
    pad_row = lambda v: jnp.pad(v, (0, AB_PAD - GDN_HEADS)).reshape(1, AB_PAD)
    ii = jnp.arange(TM_IN)
    tri = ((ii[:, None] >= ii[None, :]) & ((ii[:, None] // CHUNK) == (ii[None, :] // CHUNK))).astype(BF16)

    q, k, v, zs, ga, ybg, gbc, gbr = _in_call(
        x, mod3, norm1_w, w_all, gdn_conv_w[0], sc_conv_w[0], pad_row(gdn_a_log[0]), pad_row(gdn_dt_bias[0]),
        w_sc_out[0].astype(BF16), tri)

    wg = w_gdn_proj[0].astype(BF16)
    x1 = _gdn_call(q, k, v, zs, gbc, gbr, ga, ybg, x, mod3, gdn_norm_w, wg, w_o[0].astype(BF16))

    return _ffn_call(x1, mod3, modf3, norm2_w, normf_w.reshape(1, d), w_ffn_in[0].astype(BF16), w_ffn_out[0].astype(BF16))
```

```python
import jax
import jax.numpy as jnp
from jax import lax
from jax.experimental import pallas as pl
from jax.experimental.pallas import tpu as pltpu

D_MODEL = 1024
GDN_HEADS = 8
HEAD_DIM = 128
GDN_CONV = 4
SC_CONV = 3
CHUNK = 64
EPS = 1e-6
N_MOD = 6
D_FF = 2816
AB_PAD = 128

C_Q, C_K, C_V, C_Z, C_SCB, C_SCC, C_SCX, C_GA, C_GB, C_AB = (
    0, 1024, 2048, 3072, 4096, 5120, 6144, 7168, 8192, 9216)
W_ALL_COLS = C_AB + AB_PAD

TM_IN = 256
TM_GDN = 128
GDN_SEQS = 4
GDN_TILES = 16
GDN_SLAB = 512
TM_FFN = 512
FF_SLAB = 256
IN_SLAB = 256
CARRY = 8
REPACK_SLAB = 512

BF16 = jnp.bfloat16
F32 = jnp.float32
MIB = 1024 * 1024
VMEM_HEADROOM = 1.3


def _nbytes(shape, dtype):
    n = jnp.dtype(dtype).itemsize
    for s in shape:
        n *= s
    return n


def _vmem_limit(resident=(), pipelined=(), scratch=()):
    total = (sum(_nbytes(*b) for b in resident) + 2 * sum(_nbytes(*b) for b in pipelined)
             + sum(_nbytes(*b) for b in scratch))
    return int(-(-total * VMEM_HEADROOM // MIB)) * MIB


def _mm(a, b):
    return jnp.dot(a.astype(BF16), b.astype(BF16), preferred_element_type=F32)


def _bmm(a, b):
    return lax.dot_general(a.astype(BF16), b.astype(BF16), (((2,), (1,)), ((0,), (0,))),
                           preferred_element_type=F32)


def _bmm_nt(a, b):
    return lax.dot_general(a.astype(BF16), b.astype(BF16), (((2,), (2,)), ((0,), (0,))),
                           preferred_element_type=F32)


def _bmm_tn(a, b):
    return lax.dot_general(a.astype(BF16), b.astype(BF16), (((1,), (1,)), ((0,), (0,))),
                           preferred_element_type=F32)


def _silu(x):
    return x * jax.nn.sigmoid(x)


def _rms(x):
    return x * lax.rsqrt(jnp.mean(x * x, axis=-1, keepdims=True) + EPS)


def _mod_kernel(c_ref, w_ref, b_ref, o_ref):
    o_ref[...] = _mm(_silu(c_ref[...]), w_ref[...]) + b_ref[...]


def _mod_call(c, w, b):
    bsz, d = c.shape
    n = w.shape[1]
    tn = 1024
    return pl.pallas_call(
        _mod_kernel,
        grid=(n // tn,),
        in_specs=[pl.BlockSpec((bsz, d), lambda j: (0, 0)),
                  pl.BlockSpec((d, tn), lambda j: (0, j)),
                  pl.BlockSpec((1, tn), lambda j: (0, j))],
        out_specs=pl.BlockSpec((bsz, tn), lambda j: (0, j)),
        out_shape=jax.ShapeDtypeStruct((bsz, n), F32),
        compiler_params=pltpu.CompilerParams(
            dimension_semantics=("arbitrary",),
            vmem_limit_bytes=_vmem_limit(pipelined=[((bsz, d), F32), ((d, tn), F32), ((8, tn), F32), ((bsz, tn), F32)])),
        name="mod_rows",
    )(c, w, b.reshape(1, n))


def _repack_kernel(wt_ref, ab_ref, o_ref):
    j = pl.program_id(0)

    @pl.when(j < C_AB // REPACK_SLAB)
    def _():
        o_ref[...] = jnp.transpose(wt_ref[...]).astype(BF16)

    @pl.when(j == C_AB // REPACK_SLAB)
    def _():
        o_ref[...] = jnp.zeros(o_ref.shape, BF16)
        o_ref[:, 0:2 * GDN_HEADS] = jnp.transpose(ab_ref[...]).astype(BF16)


def _repack_call(wt):
    n, d = wt.shape
    nab = 2 * GDN_HEADS
    assert n == C_AB + nab
    sl = REPACK_SLAB
    n_main = C_AB // sl
    n_head = C_SCB // sl

    def src_row(j):
        jj = jnp.minimum(j, n_main - 1)
        return (jj * (sl // nab) + jnp.where(jj < n_head, 0, 1)) * nab

    return pl.pallas_call(
        _repack_kernel,
        grid=(n_main + 1,),
        in_specs=[pl.BlockSpec((pl.Element(sl), pl.Element(d)), lambda j: (src_row(j), 0)),
                  pl.BlockSpec((pl.Element(nab), pl.Element(d)), lambda j: (C_SCB, 0))],
        out_specs=pl.BlockSpec((d, sl), lambda j: (0, j)),
        out_shape=jax.ShapeDtypeStruct((d, W_ALL_COLS), BF16),
        compiler_params=pltpu.CompilerParams(
            dimension_semantics=("arbitrary",),
            vmem_limit_bytes=_vmem_limit(pipelined=[((sl, d), F32), ((nab, d), F32), ((d, sl), BF16)],
                                         scratch=[((d, sl), F32)])),
        name="repack_w_in",
    )(wt, wt)


def _in_kernel(x_ref, mod_ref, n1_ref, w_ref, cw_ref, scw_ref, alog_ref, dtb_ref, wsc_ref, tri_ref,
               q_ref, k_ref, v_ref, zs_ref, ga_ref, ybg_ref, gbc_ref, gbr_ref, carry, pbuf, ubuf):
    t = pl.program_id(1)
    tm = TM_IN

    @pl.when(t == 0)
    def _():
        carry[...] = jnp.zeros_like(carry)

    x = x_ref[0]
    sh1 = mod_ref[0, 0:1, :]
    sc1 = mod_ref[0, 1:2, :]
    h = (_rms(x) * n1_ref[...] * (1.0 + sc1) + sh1).astype(BF16)

    sw = IN_SLAB
    heads_per_slab = sw // HEAD_DIM

    def proj(c0, width=IN_SLAB):
        return jnp.dot(h, w_ref[:, c0:c0 + width], preferred_element_type=F32)

    def causal_conv(slot, col, p, taps, width):
        pbuf[slot, 0:CARRY, :] = carry[slot, :, col:col + sw]
        pbuf[slot, CARRY:CARRY + tm, :] = p
        acc = taps[width - 1:width, :] * p
        for j in range(width - 1):
            sh = width - 1 - j
            acc = acc + taps[j:j + 1, :] * pbuf[slot, CARRY - sh:CARRY - sh + tm, :]
        carry[slot, :, col:col + sw] = p[tm - CARRY:tm]
        return acc

    def gdn_input(slot, c0, col, out_ref, norm, scale):
        taps = cw_ref[:, c0 + col:c0 + col + sw]
        y = _silu(causal_conv(slot, col, proj(c0 + col), taps, GDN_CONV))
        for i in range(heads_per_slab):
            yh = y[:, i * HEAD_DIM:(i + 1) * HEAD_DIM]
            if norm:
                yh = yh * lax.rsqrt(jnp.sum(yh * yh, axis=-1, keepdims=True) + EPS)
                if scale != 1.0:
                    yh = yh * scale
            out_ref[0, col // HEAD_DIM + i] = yh.astype(out_ref.dtype)

    def z_gate(col):
        zs = _silu(proj(C_Z + col))
        for i in range(heads_per_slab):
            zs_ref[0, col // HEAD_DIM + i] = zs[:, i * HEAD_DIM:(i + 1) * HEAD_DIM].astype(zs_ref.dtype)

    def decay_beta():
        pab = proj(C_AB, AB_PAD)
        lane = lax.broadcasted_iota(jnp.int32, (tm, AB_PAD), 1)
        xg = pab + dtb_ref[...]
        softplus = jnp.maximum(xg, 0.0) + jnp.log1p(jnp.exp(-jnp.abs(xg)))
        g = -jnp.exp(alog_ref[...]) * softplus
        gb = jnp.where(lane < GDN_HEADS, g, jax.nn.sigmoid(pab))
        g1 = gb.astype(BF16)
        r1 = gb - g1.astype(F32)
        g2 = r1.astype(BF16)
        g3 = (r1 - g2.astype(F32)).astype(BF16)
        tri = tri_ref[...]
        cum = (jnp.dot(tri, g1, preferred_element_type=F32) + jnp.dot(tri, g2, preferred_element_type=F32)
               + jnp.dot(tri, g3, preferred_element_type=F32))
        gbc = jnp.where(lane < GDN_HEADS, cum, gb)
        gbc_ref[0] = gbc
        gbr_ref[0] = jnp.transpose(gbc)[0:2 * GDN_HEADS, :]

    def short_conv_in(col):
        prod = proj(C_SCC + col) * proj(C_SCX + col)
        conv = causal_conv(3, col, prod, scw_ref[:, col:col + sw], SC_CONV)
        ubuf[:, col:col + sw] = (proj(C_SCB + col) * conv).astype(BF16)

    def short_conv_out(col):
        yb = jnp.dot(ubuf[...], wsc_ref[:, col:col + sw], preferred_element_type=F32)
        ybg_ref[0, :, col:col + sw] = (jax.nn.sigmoid(proj(C_GB + col)) * yb).astype(ybg_ref.dtype)

    def gate_a(col):
        ga_ref[0, :, col:col + sw] = jax.nn.sigmoid(proj(C_GA + col)).astype(ga_ref.dtype)

    cols = range(0, D_MODEL, sw)
    for col in cols:
        gdn_input(0, C_Q, col, q_ref, True, HEAD_DIM ** -0.5)
        short_conv_in(col)
    for col in cols:
        gdn_input(1, C_K, col, k_ref, True, 1.0)
        gate_a(col)
        z_gate(col)
    for col in cols:
        gdn_input(2, C_V, col, v_ref, False, 1.0)
        short_conv_out(col)
    decay_beta()


def _in_call(x, mod3, norm1_w, w_all, conv_w, sc_conv_w, alog_row, dtb_row, w_sc_out, tri):
    bsz, seq, d = x.shape
    tm = TM_IN
    nt = seq // tm
    const2 = lambda b, t: (0, 0)
    resident = dict(pipeline_mode=pl.Buffered(1))
    head_spec = pl.BlockSpec((1, GDN_HEADS, tm, HEAD_DIM), lambda b, t: (b, 0, t, 0))
    tok_spec = pl.BlockSpec((1, tm, d), lambda b, t: (b, t, 0))
    head_shape = jax.ShapeDtypeStruct((bsz, GDN_HEADS, seq, HEAD_DIM), BF16)
    return pl.pallas_call(
        _in_kernel,
        grid=(bsz, nt),
        in_specs=[tok_spec,
                  pl.BlockSpec((1, N_MOD, d), lambda b, t: (b, 0, 0)),
                  pl.BlockSpec((1, d), const2),
                  pl.BlockSpec((d, W_ALL_COLS), const2, **resident),
                  pl.BlockSpec((GDN_CONV, 3 * d), const2),
                  pl.BlockSpec((SC_CONV, d), const2),
                  pl.BlockSpec((1, AB_PAD), const2),
                  pl.BlockSpec((1, AB_PAD), const2),
                  pl.BlockSpec((d, d), const2, **resident),
                  pl.BlockSpec((tm, tm), const2)],
        out_specs=[head_spec, head_spec, head_spec, head_spec, tok_spec, tok_spec,
                   pl.BlockSpec((1, tm, AB_PAD), lambda b, t: (b, t, 0)),
                   pl.BlockSpec((1, 2 * GDN_HEADS, tm), lambda b, t: (b, 0, t))],
        out_shape=[head_shape, head_shape, head_shape, head_shape,
                   jax.ShapeDtypeStruct((bsz, seq, d), BF16),
                   jax.ShapeDtypeStruct((bsz, seq, d), BF16),
                   jax.ShapeDtypeStruct((bsz, seq, AB_PAD), F32),
                   jax.ShapeDtypeStruct((bsz, 2 * GDN_HEADS, seq), F32)],
        scratch_shapes=[pltpu.VMEM((4, CARRY, d), F32),
                        pltpu.VMEM((4, CARRY + tm, IN_SLAB), F32),
                        pltpu.VMEM((tm, d), BF16)],
        compiler_params=pltpu.CompilerParams(
            dimension_semantics=("arbitrary", "arbitrary"),
            vmem_limit_bytes=_vmem_limit(
                resident=[((d, W_ALL_COLS), BF16), ((d, d), BF16)],
                pipelined=[((tm, d), F32), ((tm, tm), BF16), ((tm, AB_PAD), F32), ((2 * GDN_HEADS, tm), F32)]
                + 6 * [((tm, d), BF16)],
                scratch=[((4, CARRY, d), F32), ((4, CARRY + tm, IN_SLAB), F32), ((tm, d), BF16)])),
        name="in_proj",
    )(x, mod3, norm1_w, w_all, conv_w, sc_conv_w, alog_row, dtb_row, w_sc_out, tri)


def _block_diag(x, left):
    zero = jnp.zeros_like(x)
    return jnp.concatenate([jnp.where(left, x, zero), jnp.where(left, zero, x)], axis=1)


def _tri_inverse_stages(low, left, blk16, blk32, eye):
    def mm(x, y):
        return _bmm(x, _block_diag(y.astype(BF16), left))

    a1 = jnp.where(blk16, -low, 0.0)
    p = eye + a1
    sq = mm(a1, a1)
    yield
    for _ in range(2):
        both = mm(jnp.concatenate([p, sq], axis=1), sq)
        yield
        p = p + both[:, 0:CHUNK]
        sq = both[:, CHUNK:2 * CHUNK]
    p = p + mm(p, sq)
    yield
    for off in (jnp.where(jnp.logical_and(blk32, jnp.logical_not(blk16)), low, 0.0),
                jnp.where(blk32, 0.0, low)):
        po = mm(p, off)
        yield
        p = p - mm(po, p)
        yield
    return p


def _gdn_kernel(qn_ref, kn_ref, vn_ref, gbc_ref, gbr_ref,
                qp_ref, kp_ref, zs_ref, ga_ref, ybg_ref, x_ref, mod_ref, nw_ref, wg_ref, wo_ref,
                out_ref, state_ref, u_scr, w_scr, qk_scr, ab_scr, eb_scr, dl_scr):
    i = pl.program_id(0)
    tm = TM_GDN
    ns = GDN_SEQS
    nh = GDN_HEADS
    nb = ns * nh
    d = D_MODEL
    slot_w = lax.rem(i, 2)
    slot_r = 1 - slot_w
    t_fin = lax.rem(jnp.maximum(i - 1, 0), GDN_TILES)

    @pl.when(i == 0)
    def _():
        u_scr[1] = jnp.zeros(u_scr.shape[1:], u_scr.dtype)
        w_scr[1] = jnp.zeros(w_scr.shape[1:], w_scr.dtype)
        qk_scr[1] = jnp.zeros(qk_scr.shape[1:], qk_scr.dtype)
        ab_scr[1] = jnp.zeros(ab_scr.shape[1:], ab_scr.dtype)
        eb_scr[1] = jnp.zeros(eb_scr.shape[1:], eb_scr.dtype)
        dl_scr[1] = jnp.zeros(dl_scr.shape[1:], dl_scr.dtype)

    @pl.when(t_fin == 0)
    def _():
        state_ref[...] = jnp.zeros_like(state_ref)

    def prepare():
        gbc = gbc_ref[...]
        gbr = gbr_ref[...]

        def col_heads(j0):
            return jnp.stack([jnp.broadcast_to(gbc[sq, :, j0 + h:j0 + h + 1], (tm, HEAD_DIM))
                              for sq in range(ns) for h in range(nh)])

        def row_heads(j0):
            return jnp.stack([jnp.broadcast_to(gbr[sq, j0 + h:j0 + h + 1, :], (CHUNK, tm))
                              for sq in range(ns) for h in range(nh)])

        ri = lax.broadcasted_iota(jnp.int32, (1, CHUNK, tm), 1)
        li = lax.broadcasted_iota(jnp.int32, (1, CHUNK, tm), 2)
        left = li < CHUNK
        ci = jnp.where(left, li, li - CHUNK)
        causal = ri >= ci
        strict = ri > ci
        blk16 = (ri // 16) == (ci // 16)
        blk32 = (ri // 32) == (ci // 32)
        eye = jnp.where(ri == ci, 1.0, 0.0).astype(F32)

        def side_by_side(g):
            return jnp.where(left, g[:, 0:CHUNK, :], g[:, CHUNK:tm, :])

        q = qn_ref[...].reshape(nb, tm, HEAD_DIM)
        k = kn_ref[...].reshape(nb, tm, HEAD_DIM)
        v = vn_ref[...].reshape(nb, tm, HEAD_DIM)

        gkk = side_by_side(_bmm_nt(k, k))
        yield
        gqk = side_by_side(_bmm_nt(q, k))
        yield

        gcb = col_heads(0)
        ab = jnp.exp(gcb)
        gc2 = gcb.reshape(nb * 2, CHUNK, HEAD_DIM)
        ab_scr[slot_w] = ab.astype(ab_scr.dtype)
        eb_scr[slot_w] = jnp.exp(gc2[:, CHUNK - 1:CHUNK, :] - gc2).reshape(nb, tm, HEAD_DIM).astype(eb_scr.dtype)
        dl_scr[slot_w] = ab.reshape(nb, 2, CHUNK, HEAD_DIM)[:, :, CHUNK - 1, :]
        grow = row_heads(0)
        brow = row_heads(nh)
        barow = brow * jnp.exp(grow)

        dec = jnp.where(causal, jnp.exp(jnp.minimum(side_by_side(gcb) - grow, 0.0)), 0.0)
        bcol = side_by_side(col_heads(nh))
        low = jnp.where(strict, gkk * bcol * dec, 0.0)
        qk = jnp.where(causal, gqk * dec, 0.0).astype(qk_scr.dtype)
        qk_scr[slot_w, 0] = qk[:, :, 0:CHUNK]
        qk_scr[slot_w, 1] = qk[:, :, CHUNK:tm]
        tinv = yield from _tri_inverse_stages(low, left, blk16, blk32, eye)
        u_scr[slot_w] = _bmm(_block_diag((tinv * brow).astype(BF16), left), v)
        yield
        w_scr[slot_w] = _bmm(_block_diag((tinv * barow).astype(BF16), left), k).astype(w_scr.dtype)
        yield

    def finish():
        q = qp_ref[...].reshape(nb, tm, HEAD_DIM)
        k = kp_ref[...].reshape(nb, tm, HEAD_DIM)
        s = state_ref[...]
        outs = []
        for c in range(tm // CHUNK):
            rows = slice(c * CHUNK, (c + 1) * CHUNK)
            ws_qs = _bmm(jnp.concatenate([w_scr[slot_r, :, rows, :], q[:, rows]], axis=1), s)
            yield
            v_new = u_scr[slot_r, :, rows, :] - ws_qs[:, 0:CHUNK]
            outs.append(ab_scr[slot_r, :, rows, :].astype(F32) * ws_qs[:, CHUNK:]
                        + _bmm(qk_scr[slot_r, c], v_new))
            yield
            s = (s * dl_scr[slot_r, :, c:c + 1, :]
                 + _bmm_tn(k[:, rows], eb_scr[slot_r, :, rows, :].astype(F32) * v_new))
            yield
        state_ref[...] = s

        o = jnp.concatenate(outs, axis=1)
        on = (_rms(o) * nw_ref[...] * zs_ref[...].reshape(nb, tm, HEAD_DIM).astype(F32)).astype(BF16)
        o2d = jnp.concatenate([jnp.concatenate([on[sq * nh + h] for h in range(nh)], axis=1)
                               for sq in range(ns)], axis=0)
        ga = ga_ref[...].reshape(ns * tm, d)
        ybg = ybg_ref[...].reshape(ns * tm, d)
        merged = []
        for c0 in range(0, d, GDN_SLAB):
            ya = jnp.dot(o2d, wg_ref[:, c0:c0 + GDN_SLAB], preferred_element_type=F32)
            merged.append((ga[:, c0:c0 + GDN_SLAB].astype(F32) * ya
                           + ybg[:, c0:c0 + GDN_SLAB].astype(F32)).astype(BF16))
            yield
        merged = jnp.concatenate(merged, axis=1)
        for c0 in range(0, d, GDN_SLAB):
            mix = jnp.dot(merged, wo_ref[:, c0:c0 + GDN_SLAB], preferred_element_type=F32)
            for sq in range(ns):
                out_ref[sq, :, c0:c0 + GDN_SLAB] = (x_ref[sq, :, c0:c0 + GDN_SLAB]
                                                    + mod_ref[sq, 2:3, c0:c0 + GDN_SLAB] * mix[sq * tm:(sq + 1) * tm])
            yield

    active = [finish(), prepare()]
    while active:
        for stage in list(active):
            try:
                next(stage)
            except StopIteration:
                active.remove(stage)


def _gdn_call(q, k, v, zs, gbc, gbr, ga, ybg, x, mod3, gdn_norm_w, wg, wo):
    bsz, seq, d = x.shape
    tm = TM_GDN
    ns = GDN_SEQS
    nb = ns * GDN_HEADS
    assert seq // tm == GDN_TILES and tm == 2 * CHUNK and bsz % ns == 0
    last = (bsz // ns) * GDN_TILES - 1

    def tile(j):
        return j // GDN_TILES, lax.rem(j, GDN_TILES)

    def nxt(i):
        return tile(jnp.minimum(i, last))

    def fin(i):
        return tile(jnp.maximum(i - 1, 0))

    const2 = lambda i: (0, 0)
    resident = dict(pipeline_mode=pl.Buffered(1))
    head_blk = (ns, GDN_HEADS, tm, HEAD_DIM)
    head_nxt = pl.BlockSpec(head_blk, lambda i: (nxt(i)[0], 0, nxt(i)[1], 0))
    head_fin = pl.BlockSpec(head_blk, lambda i: (fin(i)[0], 0, fin(i)[1], 0))
    tok_fin = pl.BlockSpec((ns, tm, d), lambda i: (fin(i)[0], fin(i)[1], 0))
    scratch = [((nb, HEAD_DIM, HEAD_DIM), F32),
               ((2, nb, tm, HEAD_DIM), F32),
               ((2, nb, tm, HEAD_DIM), BF16),
               ((2, 2, nb, CHUNK, CHUNK), BF16),
               ((2, nb, tm, HEAD_DIM), BF16),
               ((2, nb, tm, HEAD_DIM), BF16),
               ((2, nb, 2, HEAD_DIM), F32)]
    return pl.pallas_call(
        _gdn_kernel,
        grid=(last + 2,),
        in_specs=[head_nxt, head_nxt, head_nxt,
                  pl.BlockSpec((ns, tm, AB_PAD), lambda i: (nxt(i)[0], nxt(i)[1], 0)),
                  pl.BlockSpec((ns, 2 * GDN_HEADS, tm), lambda i: (nxt(i)[0], 0, nxt(i)[1])),
                  head_fin, head_fin, head_fin, tok_fin, tok_fin, tok_fin,
                  pl.BlockSpec((ns, N_MOD, d), lambda i: (fin(i)[0], 0, 0)),
                  pl.BlockSpec((1, HEAD_DIM), const2),
                  pl.BlockSpec((d, d), const2, **resident),
                  pl.BlockSpec((d, d), const2, **resident)],
        out_specs=tok_fin,
        out_shape=jax.ShapeDtypeStruct((bsz, seq, d), F32),
        scratch_shapes=[pltpu.VMEM(*s) for s in scratch],
        compiler_params=pltpu.CompilerParams(
            dimension_semantics=("arbitrary",),
            vmem_limit_bytes=_vmem_limit(
                resident=[((d, d), BF16), ((d, d), BF16)],
                pipelined=6 * [(head_blk, BF16)] + 2 * [((ns, tm, d), BF16)] + 2 * [((ns, tm, d), F32)]
                + [((ns, tm, AB_PAD), F32), ((ns, 2 * GDN_HEADS, tm), F32), ((ns, 8, d), F32)],
                scratch=scratch)),
        name="gdn_mix",
    )(q, k, v, gbc, gbr, q, k, zs, ga, ybg, x, mod3, gdn_norm_w, wg, wo)


def _ffn_kernel(x_ref, mod_ref, modf_ref, n2_ref, nf_ref, win_ref, wout_ref, out_ref):
    x = x_ref[0]
    sh2 = mod_ref[0, 3:4, :]
    sc2 = mod_ref[0, 4:5, :]
    g2 = mod_ref[0, 5:6, :]
    h = (_rms(x) * n2_ref[...] * (1.0 + sc2) + sh2).astype(BF16)
    acts = []
    for s0 in range(0, D_FF, FF_SLAB):
        gate = jnp.dot(h, win_ref[:, s0:s0 + FF_SLAB], preferred_element_type=F32)
        up = jnp.dot(h, win_ref[:, D_FF + s0:D_FF + s0 + FF_SLAB], preferred_element_type=F32)
        acts.append((_silu(gate) * up).astype(BF16))
    ffn = jnp.dot(jnp.concatenate(acts, axis=1), wout_ref[...], preferred_element_type=F32)
    x2 = x + g2 * ffn
    out_ref[0] = _rms(x2) * nf_ref[...] * (1.0 + modf_ref[0, 1:2, :]) + modf_ref[0, 0:1, :]


def _ffn_call(x, mod3, modf3, norm2_w, normf_w, w_in, w_out):
    bsz, seq, d = x.shape
    tm = TM_FFN
    const2 = lambda b, t: (0, 0)
    resident = dict(pipeline_mode=pl.Buffered(1))
    tok_spec = pl.BlockSpec((1, tm, d), lambda b, t: (b, t, 0))
    return pl.pallas_call(
        _ffn_kernel,
        grid=(bsz, seq // tm),
        in_specs=[tok_spec,
                  pl.BlockSpec((1, N_MOD, d), lambda b, t: (b, 0, 0)),
                  pl.BlockSpec((1, 2, d), lambda b, t: (b, 0, 0)),
                  pl.BlockSpec((1, d), const2),
                  pl.BlockSpec((1, d), const2),
                  pl.BlockSpec((d, 2 * D_FF), const2, **resident),
                  pl.BlockSpec((D_FF, d), const2, **resident)],
        out_specs=tok_spec,
        out_shape=jax.ShapeDtypeStruct((bsz, seq, d), F32),
        compiler_params=pltpu.CompilerParams(
            dimension_semantics=("arbitrary", "arbitrary"),
            vmem_limit_bytes=_vmem_limit(resident=[((d, 2 * D_FF), BF16), ((D_FF, d), BF16)],
                                         pipelined=2 * [((tm, d), F32)] + [((8, d), F32), ((8, d), F32)],
                                         scratch=[((tm, D_FF), BF16)])),
        name="ffn_final",
    )(x, mod3, modf3, norm2_w, normf_w, w_in, w_out)


def kernel(x, c, w_ada, b_ada, norm1_w, w_in, gdn_conv_w, gdn_a_log, gdn_dt_bias, gdn_norm_w, w_gdn_proj,
           sc_conv_w, w_sc_out, w_o, norm2_w, w_ffn_in, w_ffn_out, w_ada_f, b_ada_f, normf_w):
    bsz, seq, d = x.shape
    assert w_ada.shape[0] == 1 and d == D_MODEL and seq % TM_FFN == 0

    mod3 = _mod_call(c, w_ada[0], b_ada[0]).reshape(bsz, N_MOD, d)
    modf3 = _mod_call(c, w_ada_f, b_ada_f).reshape(bsz, 2, d)

    w_all = _repack_call(jnp.transpose(w_in[0]))
    pad_row = lambda v: jnp.pad(v, (0, AB_PAD - GDN_HEADS)).reshape(1, AB_PAD)
    ii = jnp.arange(TM_IN)
    tri = ((ii[:, None] >= ii[None, :]) & ((ii[:, None] // CHUNK) == (ii[None, :] // CHUNK))).astype(BF16)

    q, k, v, zs, ga, ybg, gbc, gbr = _in_call(
        x, mod3, norm1_w, w_all, gdn_conv_w[0], sc_conv_w[0], pad_row(gdn_a_log[0]), pad_row(gdn_dt_bias[0]),
        w_sc_out[0].astype(BF16), tri)

    wg = w_gdn_proj[0].astype(BF16)
    x1 = _gdn_call(q, k, v, zs, gbc, gbr, ga, ybg, x, mod3, gdn_norm_w, wg, w_o[0].astype(BF16))

    return _ffn_call(x1, mod3, modf3, norm2_w, normf_w.reshape(1, d), w_ffn_in[0].astype(BF16), w_ffn_out[0].astype(BF16))
```

```python
import jax
import jax.numpy as jnp
from jax import lax
from jax.experimental import pallas as pl
from jax.experimental.pallas import tpu as pltpu

D_MODEL = 1024
GDN_HEADS = 8
HEAD_DIM = 128
GDN_CONV = 4
SC_CONV = 3
CHUNK = 64
EPS = 1e-6
N_MOD = 6
D_FF = 2816
AB_PAD = 128

C_Q, C_K, C_V, C_Z, C_SCB, C_SCC, C_SCX, C_GA, C_GB, C_AB = (
    0, 1024, 2048, 3072, 4096, 5120, 6144, 7168, 8192, 9216)
W_ALL_COLS = C_AB + AB_PAD

TM_IN = 256
TM_GDN = 128
GDN_SEQS = 4
GDN_TILES = 16
GDN_SLAB = 512
TM_FFN = 512
FF_SLAB = 256
IN_SLAB = 256
CARRY = 8
REPACK_SLAB = 512

BF16 = jnp.bfloat16
F32 = jnp.float32
MIB = 1024 * 1024
VMEM_HEADROOM = 1.33


def _nbytes(shape, dtype):
    n = jnp.dtype(dtype).itemsize
    for s in shape:
        n *= s
    return n


def _vmem_limit(resident=(), pipelined=(), scratch=()):
    total = (sum(_nbytes(*b) for b in resident) + 2 * sum(_nbytes(*b) for b in pipelined)
             + sum(_nbytes(*b) for b in scratch))
    return int(-(-total * VMEM_HEADROOM // MIB)) * MIB


def _mm(a, b):
    return jnp.dot(a.astype(BF16), b.astype(BF16), preferred_element_type=F32)


def _bmm(a, b):
    return lax.dot_general(a.astype(BF16), b.astype(BF16), (((2,), (1,)), ((0,), (0,))),
                           preferred_element_type=F32)


def _bmm_nt(a, b):
    return lax.dot_general(a.astype(BF16), b.astype(BF16), (((2,), (2,)), ((0,), (0,))),
                           preferred_element_type=F32)


def _bmm_tn(a, b):
    return lax.dot_general(a.astype(BF16), b.astype(BF16), (((1,), (1,)), ((0,), (0,))),
                           preferred_element_type=F32)


def _silu(x):
    return x * jax.nn.sigmoid(x)


def _rms(x):
    return x * lax.rsqrt(jnp.mean(x * x, axis=-1, keepdims=True) + EPS)


def _mod_kernel(c_ref, w_ref, b_ref, o_ref):
    o_ref[...] = _mm(_silu(c_ref[...]), w_ref[...]) + b_ref[...]


def _mod_call(c, w, b):
    bsz, d = c.shape
    n = w.shape[1]
    tn = 1024
    return pl.pallas_call(
        _mod_kernel,
        grid=(n // tn,),
        in_specs=[pl.BlockSpec((bsz, d), lambda j: (0, 0)),
                  pl.BlockSpec((d, tn), lambda j: (0, j)),
                  pl.BlockSpec((1, tn), lambda j: (0, j))],
        out_specs=pl.BlockSpec((bsz, tn), lambda j: (0, j)),
        out_shape=jax.ShapeDtypeStruct((bsz, n), F32),
        compiler_params=pltpu.CompilerParams(
            dimension_semantics=("arbitrary",),
            vmem_limit_bytes=_vmem_limit(pipelined=[((bsz, d), F32), ((d, tn), F32), ((8, tn), F32), ((bsz, tn), F32)])),
        name="mod_rows",
    )(c, w, b.reshape(1, n))


def _repack_kernel(wt_ref, ab_ref, o_ref):
    j = pl.program_id(0)

    @pl.when(j < C_AB // REPACK_SLAB)
    def _():
        o_ref[...] = jnp.transpose(wt_ref[...]).astype(BF16)

    @pl.when(j == C_AB // REPACK_SLAB)
    def _():
        o_ref[...] = jnp.zeros(o_ref.shape, BF16)
        o_ref[:, 0:2 * GDN_HEADS] = jnp.transpose(ab_ref[...]).astype(BF16)


def _repack_call(wt):
    n, d = wt.shape
    nab = 2 * GDN_HEADS
    assert n == C_AB + nab
    sl = REPACK_SLAB
    n_main = C_AB // sl
    n_head = C_SCB // sl

    def src_row(j):
        jj = jnp.minimum(j, n_main - 1)
        return (jj * (sl // nab) + jnp.where(jj < n_head, 0, 1)) * nab

    return pl.pallas_call(
        _repack_kernel,
        grid=(n_main + 1,),
        in_specs=[pl.BlockSpec((pl.Element(sl), pl.Element(d)), lambda j: (src_row(j), 0)),
                  pl.BlockSpec((pl.Element(nab), pl.Element(d)), lambda j: (C_SCB, 0))],
        out_specs=pl.BlockSpec((d, sl), lambda j: (0, j)),
        out_shape=jax.ShapeDtypeStruct((d, W_ALL_COLS), BF16),
        compiler_params=pltpu.CompilerParams(
            dimension_semantics=("arbitrary",),
            vmem_limit_bytes=_vmem_limit(pipelined=[((sl, d), F32), ((nab, d), F32), ((d, sl), BF16)],
                                         scratch=[((d, sl), F32)])),
        name="repack_w_in",
    )(wt, wt)


def _in_kernel(x_ref, mod_ref, n1_ref, w_ref, cw_ref, scw_ref, alog_ref, dtb_ref, wsc_ref, tri_ref,
               q_ref, k_ref, v_ref, zs_ref, ga_ref, ybg_ref, gbc_ref, gbr_ref, carry, pbuf, ubuf):
    t = pl.program_id(1)
    tm = TM_IN

    @pl.when(t == 0)
    def _():
        carry[...] = jnp.zeros_like(carry)

    x = x_ref[0]
    sh1 = mod_ref[0, 0:1, :]
    sc1 = mod_ref[0, 1:2, :]
    h = (_rms(x) * n1_ref[...] * (1.0 + sc1) + sh1).astype(BF16)

    sw = IN_SLAB
    heads_per_slab = sw // HEAD_DIM

    def proj(c0, width=IN_SLAB):
        return jnp.dot(h, w_ref[:, c0:c0 + width], preferred_element_type=F32)

    def causal_conv(slot, col, p, taps, width):
        pbuf[slot, 0:CARRY, :] = carry[slot, :, col:col + sw]
        pbuf[slot, CARRY:CARRY + tm, :] = p
        acc = taps[width - 1:width, :] * p
        for j in range(width - 1):
            sh = width - 1 - j
            acc = acc + taps[j:j + 1, :] * pbuf[slot, CARRY - sh:CARRY - sh + tm, :]
        carry[slot, :, col:col + sw] = p[tm - CARRY:tm]
        return acc

    def gdn_input(slot, c0, col, out_ref, norm, scale):
        taps = cw_ref[:, c0 + col:c0 + col + sw]
        y = _silu(causal_conv(slot, col, proj(c0 + col), taps, GDN_CONV))
        for i in range(heads_per_slab):
            yh = y[:, i * HEAD_DIM:(i + 1) * HEAD_DIM]
            if norm:
                yh = yh * lax.rsqrt(jnp.sum(yh * yh, axis=-1, keepdims=True) + EPS)
                if scale != 1.0:
                    yh = yh * scale
            out_ref[0, col // HEAD_DIM + i] = yh.astype(out_ref.dtype)

    def z_gate(col):
        zs = _silu(proj(C_Z + col))
        for i in range(heads_per_slab):
            zs_ref[0, col // HEAD_DIM + i] = zs[:, i * HEAD_DIM:(i + 1) * HEAD_DIM].astype(zs_ref.dtype)

    def decay_beta():
        pab = proj(C_AB, AB_PAD)
        lane = lax.broadcasted_iota(jnp.int32, (tm, AB_PAD), 1)
        xg = pab + dtb_ref[...]
        softplus = jnp.maximum(xg, 0.0) + jnp.log1p(jnp.exp(-jnp.abs(xg)))
        g = -jnp.exp(alog_ref[...]) * softplus
        gb = jnp.where(lane < GDN_HEADS, g, jax.nn.sigmoid(pab))
        g1 = gb.astype(BF16)
        r1 = gb - g1.astype(F32)
        g2 = r1.astype(BF16)
        g3 = (r1 - g2.astype(F32)).astype(BF16)
        tri = tri_ref[...]
        cum = (jnp.dot(tri, g1, preferred_element_type=F32) + jnp.dot(tri, g2, preferred_element_type=F32)
               + jnp.dot(tri, g3, preferred_element_type=F32))
        gbc = jnp.where(lane < GDN_HEADS, cum, gb)
        gbc_ref[0] = gbc
        gbr_ref[0] = jnp.transpose(gbc)[0:2 * GDN_HEADS, :]

    def short_conv_in(col):
        prod = proj(C_SCC + col) * proj(C_SCX + col)
        conv = causal_conv(3, col, prod, scw_ref[:, col:col + sw], SC_CONV)
        ubuf[:, col:col + sw] = (proj(C_SCB + col) * conv).astype(BF16)

    def short_conv_out(col):
        yb = jnp.dot(ubuf[...], wsc_ref[:, col:col + sw], preferred_element_type=F32)
        ybg_ref[0, :, col:col + sw] = (jax.nn.sigmoid(proj(C_GB + col)) * yb).astype(ybg_ref.dtype)

    def gate_a(col):
        ga_ref[0, :, col:col + sw] = jax.nn.sigmoid(proj(C_GA + col)).astype(ga_ref.dtype)

    cols = range(0, D_MODEL, sw)
    for col in cols:
        gdn_input(0, C_Q, col, q_ref, True, HEAD_DIM ** -0.5)
        short_conv_in(col)
    for col in cols:
        gdn_input(1, C_K, col, k_ref, True, 1.0)
        gate_a(col)
        z_gate(col)
    for col in cols:
        gdn_input(2, C_V, col, v_ref, False, 1.0)
        short_conv_out(col)
    decay_beta()


def _in_call(x, mod3, norm1_w, w_all, conv_w, sc_conv_w, alog_row, dtb_row, w_sc_out, tri):
    bsz, seq, d = x.shape
    tm = TM_IN
    nt = seq // tm
    const2 = lambda b, t: (0, 0)
    resident = dict(pipeline_mode=pl.Buffered(1))
    head_spec = pl.BlockSpec((1, GDN_HEADS, tm, HEAD_DIM), lambda b, t: (b, 0, t, 0))
    tok_spec = pl.BlockSpec((1, tm, d), lambda b, t: (b, t, 0))
    head_shape = jax.ShapeDtypeStruct((bsz, GDN_HEADS, seq, HEAD_DIM), BF16)
    return pl.pallas_call(
        _in_kernel,
        grid=(bsz, nt),
        in_specs=[tok_spec,
                  pl.BlockSpec((1, N_MOD, d), lambda b, t: (b, 0, 0)),
                  pl.BlockSpec((1, d), const2),
                  pl.BlockSpec((d, W_ALL_COLS), const2, **resident),
                  pl.BlockSpec((GDN_CONV, 3 * d), const2),
                  pl.BlockSpec((SC_CONV, d), const2),
                  pl.BlockSpec((1, AB_PAD), const2),
                  pl.BlockSpec((1, AB_PAD), const2),
                  pl.BlockSpec((d, d), const2, **resident),
                  pl.BlockSpec((tm, tm), const2)],
        out_specs=[head_spec, head_spec, head_spec, head_spec, tok_spec, tok_spec,
                   pl.BlockSpec((1, tm, AB_PAD), lambda b, t: (b, t, 0)),
                   pl.BlockSpec((1, 2 * GDN_HEADS, tm), lambda b, t: (b, 0, t))],
        out_shape=[head_shape, head_shape, head_shape, head_shape,
                   jax.ShapeDtypeStruct((bsz, seq, d), BF16),
                   jax.ShapeDtypeStruct((bsz, seq, d), BF16),
                   jax.ShapeDtypeStruct((bsz, seq, AB_PAD), F32),
                   jax.ShapeDtypeStruct((bsz, 2 * GDN_HEADS, seq), F32)],
        scratch_shapes=[pltpu.VMEM((4, CARRY, d), F32),
                        pltpu.VMEM((4, CARRY + tm, IN_SLAB), F32),
                        pltpu.VMEM((tm, d), BF16)],
        compiler_params=pltpu.CompilerParams(
            dimension_semantics=("arbitrary", "arbitrary"),
            vmem_limit_bytes=_vmem_limit(
                resident=[((d, W_ALL_COLS), BF16), ((d, d), BF16)],
                pipelined=[((tm, d), F32), ((tm, tm), BF16), ((tm, AB_PAD), F32), ((2 * GDN_HEADS, tm), F32)]
                + 6 * [((tm, d), BF16)],
                scratch=[((4, CARRY, d), F32), ((4, CARRY + tm, IN_SLAB), F32), ((tm, d), BF16)])),
        name="in_proj",
    )(x, mod3, norm1_w, w_all, conv_w, sc_conv_w, alog_row, dtb_row, w_sc_out, tri)


def _block_diag(x, left):
    zero = jnp.zeros_like(x)
    return jnp.concatenate([jnp.where(left, x, zero), jnp.where(left, zero, x)], axis=1)


def _tri_inverse_stages(low, left, blk16, blk32, eye):
    def mm(x, y):
        return _bmm(x, _block_diag(y.astype(BF16), left))

    a1 = jnp.where(blk16, -low, 0.0)
    p = eye + a1
    sq = mm(a1, a1)
    yield
    for _ in range(2):
        both = mm(jnp.concatenate([p, sq], axis=1), sq)
        yield
        p = p + both[:, 0:CHUNK]
        sq = both[:, CHUNK:2 * CHUNK]
    p = p + mm(p, sq)
    yield
    for off in (jnp.where(jnp.logical_and(blk32, jnp.logical_not(blk16)), low, 0.0),
                jnp.where(blk32, 0.0, low)):
        po = mm(p, off)
        yield
        p = p - mm(po, p)
        yield
    return p


def _gdn_kernel(qn_ref, kn_ref, vn_ref, gbc_ref, gbr_ref,
                qp_ref, kp_ref, zs_ref, ga_ref, ybg_ref, x_ref, mod_ref, nw_ref, wg_ref, wo_ref,
                out_ref, state_ref, u_scr, w_scr, qk_scr, ab_scr, eb_scr, dl_scr):
    i = pl.program_id(0)
    tm = TM_GDN
    ns = GDN_SEQS
    nh = GDN_HEADS
    nb = ns * nh
    d = D_MODEL
    slot_w = lax.rem(i, 2)
    slot_r = 1 - slot_w
    t_fin = lax.rem(jnp.maximum(i - 1, 0), GDN_TILES)

    @pl.when(i == 0)
    def _():
        u_scr[1] = jnp.zeros(u_scr.shape[1:], u_scr.dtype)
        w_scr[1] = jnp.zeros(w_scr.shape[1:], w_scr.dtype)
        qk_scr[1] = jnp.zeros(qk_scr.shape[1:], qk_scr.dtype)
        ab_scr[1] = jnp.zeros(ab_scr.shape[1:], ab_scr.dtype)
        eb_scr[1] = jnp.zeros(eb_scr.shape[1:], eb_scr.dtype)
        dl_scr[1] = jnp.zeros(dl_scr.shape[1:], dl_scr.dtype)

    @pl.when(t_fin == 0)
    def _():
        state_ref[...] = jnp.zeros_like(state_ref)

    def prepare():
        gbc = gbc_ref[...]
        gbr = gbr_ref[...]

        def col_heads(j0):
            return jnp.stack([jnp.broadcast_to(gbc[sq, :, j0 + h:j0 + h + 1], (tm, HEAD_DIM))
                              for sq in range(ns) for h in range(nh)])

        def row_heads(j0):
            return jnp.stack([jnp.broadcast_to(gbr[sq, j0 + h:j0 + h + 1, :], (CHUNK, tm))
                              for sq in range(ns) for h in range(nh)])

        ri = lax.broadcasted_iota(jnp.int32, (1, CHUNK, tm), 1)
        li = lax.broadcasted_iota(jnp.int32, (1, CHUNK, tm), 2)
        left = li < CHUNK
        ci = jnp.where(left, li, li - CHUNK)
        causal = ri >= ci
        strict = ri > ci
        blk16 = (ri // 16) == (ci // 16)
        blk32 = (ri // 32) == (ci // 32)
        eye = jnp.where(ri == ci, 1.0, 0.0).astype(F32)

        def side_by_side(g):
            return jnp.where(left, g[:, 0:CHUNK, :], g[:, CHUNK:tm, :])

        q = qn_ref[...].reshape(nb, tm, HEAD_DIM)
        k = kn_ref[...].reshape(nb, tm, HEAD_DIM)
        v = vn_ref[...].reshape(nb, tm, HEAD_DIM)

        gkk = side_by_side(_bmm_nt(k, k))
        yield
        gqk = side_by_side(_bmm_nt(q, k))
        yield

        gcb = col_heads(0)
        ab = jnp.exp(gcb)
        gc2 = gcb.reshape(nb * 2, CHUNK, HEAD_DIM)
        ab_scr[slot_w] = ab.astype(ab_scr.dtype)
        eb_scr[slot_w] = jnp.exp(gc2[:, CHUNK - 1:CHUNK, :] - gc2).reshape(nb, tm, HEAD_DIM).astype(eb_scr.dtype)
        dl_scr[slot_w] = ab.reshape(nb, 2, CHUNK, HEAD_DIM)[:, :, CHUNK - 1, :]
        grow = row_heads(0)
        brow = row_heads(nh)
        barow = brow * jnp.exp(grow)

        dec = jnp.where(causal, jnp.exp(jnp.minimum(side_by_side(gcb) - grow, 0.0)), 0.0)
        bcol = side_by_side(col_heads(nh))
        low = jnp.where(strict, gkk * bcol * dec, 0.0)
        qk = jnp.where(causal, gqk * dec, 0.0).astype(qk_scr.dtype)
        qk_scr[slot_w, 0] = qk[:, :, 0:CHUNK]
        qk_scr[slot_w, 1] = qk[:, :, CHUNK:tm]
        tinv = yield from _tri_inverse_stages(low, left, blk16, blk32, eye)
        u_scr[slot_w] = _bmm(_block_diag((tinv * brow).astype(BF16), left), v)
        yield
        w_scr[slot_w] = _bmm(_block_diag((tinv * barow).astype(BF16), left), k).astype(w_scr.dtype)
        yield

    def finish():
        q = qp_ref[...].reshape(nb, tm, HEAD_DIM)
        k = kp_ref[...].reshape(nb, tm, HEAD_DIM)
        s = state_ref[...]
        outs = []
        for c in range(tm // CHUNK):
            rows = slice(c * CHUNK, (c + 1) * CHUNK)
            ws_qs = _bmm(jnp.concatenate([w_scr[slot_r, :, rows, :], q[:, rows]], axis=1), s)
            yield
            v_new = u_scr[slot_r, :, rows, :] - ws_qs[:, 0:CHUNK]
            outs.append(ab_scr[slot_r, :, rows, :].astype(F32) * ws_qs[:, CHUNK:]
                        + _bmm(qk_scr[slot_r, c], v_new))
            yield
            s = (s * dl_scr[slot_r, :, c:c + 1, :]
                 + _bmm_tn(k[:, rows], eb_scr[slot_r, :, rows, :].astype(F32) * v_new))
            yield
        state_ref[...] = s

        o = jnp.concatenate(outs, axis=1)
        on = (_rms(o) * nw_ref[...] * zs_ref[...].reshape(nb, tm, HEAD_DIM).astype(F32)).astype(BF16)
        o2d = jnp.concatenate([jnp.concatenate([on[sq * nh + h] for h in range(nh)], axis=1)
                               for sq in range(ns)], axis=0)
        ga = ga_ref[...].reshape(ns * tm, d)
        ybg = ybg_ref[...].reshape(ns * tm, d)
        merged = []
        for c0 in range(0, d, GDN_SLAB):
            ya = jnp.dot(o2d, wg_ref[:, c0:c0 + GDN_SLAB], preferred_element_type=F32)
            merged.append((ga[:, c0:c0 + GDN_SLAB].astype(F32) * ya
                           + ybg[:, c0:c0 + GDN_SLAB].astype(F32)).astype(BF16))
            yield
        merged = jnp.concatenate(merged, axis=1)
        for c0 in range(0, d, GDN_SLAB):
            mix = jnp.dot(merged, wo_ref[:, c0:c0 + GDN_SLAB], preferred_element_type=F32)
            for sq in range(ns):
                out_ref[sq, :, c0:c0 + GDN_SLAB] = (x_ref[sq, :, c0:c0 + GDN_SLAB]
                                                    + mod_ref[sq, 2:3, c0:c0 + GDN_SLAB] * mix[sq * tm:(sq + 1) * tm])
            yield

    active = [finish(), prepare()]
    while active:
        for stage in list(active):
            try:
                next(stage)
            except StopIteration:
                active.remove(stage)


def _gdn_call(q, k, v, zs, gbc, gbr, ga, ybg, x, mod3, gdn_norm_w, wg, wo):
    bsz, seq, d = x.shape
    tm = TM_GDN
    ns = GDN_SEQS
    nb = ns * GDN_HEADS
    assert seq // tm == GDN_TILES and tm == 2 * CHUNK and bsz % ns == 0
    last = (bsz // ns) * GDN_TILES - 1

    def tile(j):
        return j // GDN_TILES, lax.rem(j, GDN_TILES)

    def nxt(i):
        return tile(jnp.minimum(i, last))

    def fin(i):
        return tile(jnp.maximum(i - 1, 0))

    const2 = lambda i: (0, 0)
    resident = dict(pipeline_mode=pl.Buffered(1))
    head_blk = (ns, GDN_HEADS, tm, HEAD_DIM)
    head_nxt = pl.BlockSpec(head_blk, lambda i: (nxt(i)[0], 0, nxt(i)[1], 0))
    head_fin = pl.BlockSpec(head_blk, lambda i: (fin(i)[0], 0, fin(i)[1], 0))
    tok_fin = pl.BlockSpec((ns, tm, d), lambda i: (fin(i)[0], fin(i)[1], 0))
    scratch = [((nb, HEAD_DIM, HEAD_DIM), F32),
               ((2, nb, tm, HEAD_DIM), F32),
               ((2, nb, tm, HEAD_DIM), BF16),
               ((2, 2, nb, CHUNK, CHUNK), BF16),
               ((2, nb, tm, HEAD_DIM), BF16),
               ((2, nb, tm, HEAD_DIM), BF16),
               ((2, nb, 2, HEAD_DIM), F32)]
    return pl.pallas_call(
        _gdn_kernel,
        grid=(last + 2,),
        in_specs=[head_nxt, head_nxt, head_nxt,
                  pl.BlockSpec((ns, tm, AB_PAD), lambda i: (nxt(i)[0], nxt(i)[1], 0)),
                  pl.BlockSpec((ns, 2 * GDN_HEADS, tm), lambda i: (nxt(i)[0], 0, nxt(i)[1])),
                  head_fin, head_fin, head_fin, tok_fin, tok_fin, tok_fin,
                  pl.BlockSpec((ns, N_MOD, d), lambda i: (fin(i)[0], 0, 0)),
                  pl.BlockSpec((1, HEAD_DIM), const2),
                  pl.BlockSpec((d, d), const2, **resident),
                  pl.BlockSpec((d, d), const2, **resident)],
        out_specs=tok_fin,
        out_shape=jax.ShapeDtypeStruct((bsz, seq, d), F32),
        scratch_shapes=[pltpu.VMEM(*s) for s in scratch],
        compiler_params=pltpu.CompilerParams(
            dimension_semantics=("arbitrary",),
            vmem_limit_bytes=_vmem_limit(
                resident=[((d, d), BF16), ((d, d), BF16)],
                pipelined=6 * [(head_blk, BF16)] + 2 * [((ns, tm, d), BF16)] + 2 * [((ns, tm, d), F32)]
                + [((ns, tm, AB_PAD), F32), ((ns, 2 * GDN_HEADS, tm), F32), ((ns, 8, d), F32)],
                scratch=scratch)),
        name="gdn_mix",
    )(q, k, v, gbc, gbr, q, k, zs, ga, ybg, x, mod3, gdn_norm_w, wg, wo)


def _ffn_kernel(x_ref, mod_ref, modf_ref, n2_ref, nf_ref, win_ref, wout_ref, out_ref):
    x = x_ref[0]
    sh2 = mod_ref[0, 3:4, :]
    sc2 = mod_ref[0, 4:5, :]
    g2 = mod_ref[0, 5:6, :]
    h = (_rms(x) * n2_ref[...] * (1.0 + sc2) + sh2).astype(BF16)
    acts = []
    for s0 in range(0, D_FF, FF_SLAB):
        gate = jnp.dot(h, win_ref[:, s0:s0 + FF_SLAB], preferred_element_type=F32)
        up = jnp.dot(h, win_ref[:, D_FF + s0:D_FF + s0 + FF_SLAB], preferred_element_type=F32)
        acts.append((_silu(gate) * up).astype(BF16))
    ffn = jnp.dot(jnp.concatenate(acts, axis=1), wout_ref[...], preferred_element_type=F32)
    x2 = x + g2 * ffn
    out_ref[0] = _rms(x2) * nf_ref[...] * (1.0 + modf_ref[0, 1:2, :]) + modf_ref[0, 0:1, :]


def _ffn_call(x, mod3, modf3, norm2_w, normf_w, w_in, w_out):
    bsz, seq, d = x.shape
    tm = TM_FFN
    const2 = lambda b, t: (0, 0)
    resident = dict(pipeline_mode=pl.Buffered(1))
    tok_spec = pl.BlockSpec((1, tm, d), lambda b, t: (b, t, 0))
    return pl.pallas_call(
        _ffn_kernel,
        grid=(bsz, seq // tm),
        in_specs=[tok_spec,
                  pl.BlockSpec((1, N_MOD, d), lambda b, t: (b, 0, 0)),
                  pl.BlockSpec((1, 2, d), lambda b, t: (b, 0, 0)),
                  pl.BlockSpec((1, d), const2),
                  pl.BlockSpec((1, d), const2),
                  pl.BlockSpec((d, 2 * D_FF), const2, **resident),
                  pl.BlockSpec((D_FF, d), const2, **resident)],
        out_specs=tok_spec,
        out_shape=jax.ShapeDtypeStruct((bsz, seq, d), F32),
        compiler_params=pltpu.CompilerParams(
            dimension_semantics=("arbitrary", "arbitrary"),
            vmem_limit_bytes=_vmem_limit(resident=[((d, 2 * D_FF), BF16), ((D_FF, d), BF16)],
                                         pipelined=2 * [((tm, d), F32)] + [((8, d), F32), ((8, d), F32)],
                                         scratch=[((tm, D_FF), BF16)])),
        name="ffn_final",
    )(x, mod3, modf3, norm2_w, normf_w, w_in, w_out)


def kernel(x, c, w_ada, b_ada, norm1_w, w_in, gdn_conv_w, gdn_a_log, gdn_dt_bias, gdn_norm_w, w_gdn_proj,
           sc_conv_w, w_sc_out, w_o, norm2_w, w_ffn_in, w_ffn_out, w_ada_f, b_ada_f, normf_w):
    bsz, seq, d = x.shape
    assert w_ada.shape[0] == 1 and d == D_MODEL and seq % TM_FFN == 0

    mod3 = _mod_call(c, w_ada[0], b_ada[0]).reshape(bsz, N_MOD, d)
    modf3 = _mod_call(c, w_ada_f, b_ada_f).reshape(bsz, 2, d)

    w_all = _repack_call(jnp.transpose(w_in[0]))
    pad_row = lambda v: jnp.pad(v, (0, AB_PAD - GDN_HEADS)).reshape(1, AB_PAD)
    ii = jnp.arange(TM_IN)
    tri = ((ii[:, None] >= ii[None, :]) & ((ii[:, None] // CHUNK) == (ii[None, :] // CHUNK))).astype(BF16)

    q, k, v, zs, ga, ybg, gbc, gbr = _in_call(
        x, mod3, norm1_w, w_all, gdn_conv_w[0], sc_conv_w[0], pad_row(gdn_a_log[0]), pad_row(gdn_dt_bias[0]),
        w_sc_out[0].astype(BF16), tri)

    wg = w_gdn_proj[0].astype(BF16)
    x1 = _gdn_call(q, k, v, zs, gbc, gbr, ga, ybg, x, mod3, gdn_norm_w, wg, w_o[0].astype(BF16))

    return _ffn_call(x1, mod3, modf3, norm2_w, normf_w.reshape(1, d), w_ffn_in[0].astype(BF16), w_ffn_out[0].astype(BF16))
```

```python
import jax
import jax.numpy as jnp
from jax import lax
from jax.experimental import pallas as pl
from jax.experimental.pallas import tpu as pltpu

D_MODEL = 1024
GDN_HEADS = 8
HEAD_DIM = 128
GDN_CONV = 4
SC_CONV = 3
CHUNK = 64
EPS = 1e-6
N_MOD = 6
D_FF = 2816
AB_PAD = 128

C_Q, C_K, C_V, C_Z, C_SCB, C_SCC, C_SCX, C_GA, C_GB, C_AB = (
    0, 1024, 2048, 3072, 4096, 5120, 6144, 7168, 8192, 9216)
W_ALL_COLS = C_AB + AB_PAD

TM_IN = 256
TM_GDN = 128
GDN_SEQS = 4
GDN_TILES = 16
GDN_SLAB = 512
TM_FFN = 512
FF_SLAB = 256
IN_SLAB = 256
CARRY = 8
REPACK_SLAB = 512

BF16 = jnp.bfloat16
F32 = jnp.float32
MIB = 1024 * 1024
VMEM_HEADROOM = 1.33


def _nbytes(shape, dtype):
    n = jnp.dtype(dtype).itemsize
    for s in shape:
        n *= s
    return n


def _vmem_limit(resident=(), pipelined=(), scratch=()):
    total = (sum(_nbytes(*b) for b in resident) + 2 * sum(_nbytes(*b) for b in pipelined)
             + sum(_nbytes(*b) for b in scratch))
    return int(-(-total * VMEM_HEADROOM // MIB)) * MIB


def _mm(a, b):
    return jnp.dot(a.astype(BF16), b.astype(BF16), preferred_element_type=F32)


def _bmm(a, b):
    return lax.dot_general(a.astype(BF16), b.astype(BF16), (((2,), (1,)), ((0,), (0,))),
                           preferred_element_type=F32)


def _bmm_nt(a, b):
    return lax.dot_general(a.astype(BF16), b.astype(BF16), (((2,), (2,)), ((0,), (0,))),
                           preferred_element_type=F32)


def _bmm_tn(a, b):
    return lax.dot_general(a.astype(BF16), b.astype(BF16), (((1,), (1,)), ((0,), (0,))),
                           preferred_element_type=F32)


def _silu(x):
    return x * jax.nn.sigmoid(x)


def _rms(x):
    return x * lax.rsqrt(jnp.mean(x * x, axis=-1, keepdims=True) + EPS)


def _mod_kernel(c_ref, w_ref, b_ref, o_ref):
    o_ref[...] = _mm(_silu(c_ref[...]), w_ref[...]) + b_ref[...]


def _mod_call(c, w, b):
    bsz, d = c.shape
    n = w.shape[1]
    tn = 1024
    return pl.pallas_call(
        _mod_kernel,
        grid=(n // tn,),
        in_specs=[pl.BlockSpec((bsz, d), lambda j: (0, 0)),
                  pl.BlockSpec((d, tn), lambda j: (0, j)),
                  pl.BlockSpec((1, tn), lambda j: (0, j))],
        out_specs=pl.BlockSpec((bsz, tn), lambda j: (0, j)),
        out_shape=jax.ShapeDtypeStruct((bsz, n), F32),
        compiler_params=pltpu.CompilerParams(
            dimension_semantics=("arbitrary",),
            vmem_limit_bytes=_vmem_limit(pipelined=[((bsz, d), F32), ((d, tn), F32), ((8, tn), F32), ((bsz, tn), F32)])),
        name="mod_rows",
    )(c, w, b.reshape(1, n))


def _repack_kernel(wt_ref, ab_ref, o_ref):
    j = pl.program_id(0)

    @pl.when(j < C_AB // REPACK_SLAB)
    def _():
        o_ref[...] = jnp.transpose(wt_ref[...]).astype(BF16)

    @pl.when(j == C_AB // REPACK_SLAB)
    def _():
        o_ref[...] = jnp.zeros(o_ref.shape, BF16)
        o_ref[:, 0:2 * GDN_HEADS] = jnp.transpose(ab_ref[...]).astype(BF16)


def _repack_call(wt):
    n, d = wt.shape
    nab = 2 * GDN_HEADS
    assert n == C_AB + nab
    sl = REPACK_SLAB
    n_main = C_AB // sl
    n_head = C_SCB // sl

    def src_row(j):
        jj = jnp.minimum(j, n_main - 1)
        return (jj * (sl // nab) + jnp.where(jj < n_head, 0, 1)) * nab

    return pl.pallas_call(
        _repack_kernel,
        grid=(n_main + 1,),
        in_specs=[pl.BlockSpec((pl.Element(sl), pl.Element(d)), lambda j: (src_row(j), 0)),
                  pl.BlockSpec((pl.Element(nab), pl.Element(d)), lambda j: (C_SCB, 0))],
        out_specs=pl.BlockSpec((d, sl), lambda j: (0, j)),
        out_shape=jax.ShapeDtypeStruct((d, W_ALL_COLS), BF16),
        compiler_params=pltpu.CompilerParams(
            dimension_semantics=("arbitrary",),
            vmem_limit_bytes=_vmem_limit(pipelined=[((sl, d), F32), ((nab, d), F32), ((d, sl), BF16)],
                                         scratch=[((d, sl), F32)])),
        name="repack_w_in",
    )(wt, wt)


def _in_kernel(x_ref, mod_ref, n1_ref, w_ref, cw_ref, scw_ref, alog_ref, dtb_ref, wsc_ref, tri_ref,
               q_ref, k_ref, v_ref, zs_ref, ga_ref, ybg_ref, gbc_ref, gbr_ref, carry, pbuf, ubuf):
    t = pl.program_id(1)
    tm = TM_IN

    @pl.when(t == 0)
    def _():
        carry[...] = jnp.zeros_like(carry)

    x = x_ref[0]
    sh1 = mod_ref[0, 0:1, :]
    sc1 = mod_ref[0, 1:2, :]
    h = (_rms(x) * n1_ref[...] * (1.0 + sc1) + sh1).astype(BF16)

    sw = IN_SLAB
    heads_per_slab = sw // HEAD_DIM

    def proj(c0, width=IN_SLAB):
        return jnp.dot(h, w_ref[:, c0:c0 + width], preferred_element_type=F32)

    def causal_conv(slot, col, p, taps, width):
        pbuf[slot, 0:CARRY, :] = carry[slot, :, col:col + sw]
        pbuf[slot, CARRY:CARRY + tm, :] = p
        acc = taps[width - 1:width, :] * p
        for j in range(width - 1):
            sh = width - 1 - j
            acc = acc + taps[j:j + 1, :] * pbuf[slot, CARRY - sh:CARRY - sh + tm, :]
        carry[slot, :, col:col + sw] = p[tm - CARRY:tm]
        return acc

    def gdn_input(slot, c0, col, out_ref, norm, scale):
        taps = cw_ref[:, c0 + col:c0 + col + sw]
        y = _silu(causal_conv(slot, col, proj(c0 + col), taps, GDN_CONV))
        for i in range(heads_per_slab):
            yh = y[:, i * HEAD_DIM:(i + 1) * HEAD_DIM]
            if norm:
                yh = yh * lax.rsqrt(jnp.sum(yh * yh, axis=-1, keepdims=True) + EPS)
                if scale != 1.0:
                    yh = yh * scale
            out_ref[0, col // HEAD_DIM + i] = yh.astype(out_ref.dtype)

    def z_gate(col):
        zs = _silu(proj(C_Z + col))
        for i in range(heads_per_slab):
            zs_ref[0, col // HEAD_DIM + i] = zs[:, i * HEAD_DIM:(i + 1) * HEAD_DIM].astype(zs_ref.dtype)

    def decay_beta():
        pab = proj(C_AB, AB_PAD)
        lane = lax.broadcasted_iota(jnp.int32, (tm, AB_PAD), 1)
        xg = pab + dtb_ref[...]
        softplus = jnp.maximum(xg, 0.0) + jnp.log1p(jnp.exp(-jnp.abs(xg)))
        g = -jnp.exp(alog_ref[...]) * softplus
        gb = jnp.where(lane < GDN_HEADS, g, jax.nn.sigmoid(pab))
        g1 = gb.astype(BF16)
        r1 = gb - g1.astype(F32)
        g2 = r1.astype(BF16)
        g3 = (r1 - g2.astype(F32)).astype(BF16)
        tri = tri_ref[...]
        cum = (jnp.dot(tri, g1, preferred_element_type=F32) + jnp.dot(tri, g2, preferred_element_type=F32)
               + jnp.dot(tri, g3, preferred_element_type=F32))
        gbc = jnp.where(lane < GDN_HEADS, cum, gb)
        gbc_ref[0] = gbc
        gbr_ref[0] = jnp.transpose(gbc)[0:2 * GDN_HEADS, :]

    def short_conv_in(col):
        prod = proj(C_SCC + col) * proj(C_SCX + col)
        conv = causal_conv(3, col, prod, scw_ref[:, col:col + sw], SC_CONV)
        ubuf[:, col:col + sw] = (proj(C_SCB + col) * conv).astype(BF16)

    def short_conv_out(col):
        yb = jnp.dot(ubuf[...], wsc_ref[:, col:col + sw], preferred_element_type=F32)
        ybg_ref[0, :, col:col + sw] = (jax.nn.sigmoid(proj(C_GB + col)) * yb).astype(ybg_ref.dtype)

    def gate_a(col):
        ga_ref[0, :, col:col + sw] = jax.nn.sigmoid(proj(C_GA + col)).astype(ga_ref.dtype)

    cols = range(0, D_MODEL, sw)
    for col in cols:
        gdn_input(0, C_Q, col, q_ref, True, HEAD_DIM ** -0.5)
        short_conv_in(col)
    decay_beta()
    for col in cols:
        gdn_input(1, C_K, col, k_ref, True, 1.0)
        gate_a(col)
        z_gate(col)
    for col in cols:
        gdn_input(2, C_V, col, v_ref, False, 1.0)
        short_conv_out(col)


def _in_call(x, mod3, norm1_w, w_all, conv_w, sc_conv_w, alog_row, dtb_row, w_sc_out, tri):
    bsz, seq, d = x.shape
    tm = TM_IN
    nt = seq // tm
    const2 = lambda b, t: (0, 0)
    resident = dict(pipeline_mode=pl.Buffered(1))
    head_spec = pl.BlockSpec((1, GDN_HEADS, tm, HEAD_DIM), lambda b, t: (b, 0, t, 0))
    tok_spec = pl.BlockSpec((1, tm, d), lambda b, t: (b, t, 0))
    head_shape = jax.ShapeDtypeStruct((bsz, GDN_HEADS, seq, HEAD_DIM), BF16)
    return pl.pallas_call(
        _in_kernel,
        grid=(bsz, nt),
        in_specs=[tok_spec,
                  pl.BlockSpec((1, N_MOD, d), lambda b, t: (b, 0, 0)),
                  pl.BlockSpec((1, d), const2),
                  pl.BlockSpec((d, W_ALL_COLS), const2, **resident),
                  pl.BlockSpec((GDN_CONV, 3 * d), const2),
                  pl.BlockSpec((SC_CONV, d), const2),
                  pl.BlockSpec((1, AB_PAD), const2),
                  pl.BlockSpec((1, AB_PAD), const2),
                  pl.BlockSpec((d, d), const2, **resident),
                  pl.BlockSpec((tm, tm), const2)],
        out_specs=[head_spec, head_spec, head_spec, head_spec, tok_spec, tok_spec,
                   pl.BlockSpec((1, tm, AB_PAD), lambda b, t: (b, t, 0)),
                   pl.BlockSpec((1, 2 * GDN_HEADS, tm), lambda b, t: (b, 0, t))],
        out_shape=[head_shape, head_shape, head_shape, head_shape,
                   jax.ShapeDtypeStruct((bsz, seq, d), BF16),
                   jax.ShapeDtypeStruct((bsz, seq, d), BF16),
                   jax.ShapeDtypeStruct((bsz, seq, AB_PAD), F32),
                   jax.ShapeDtypeStruct((bsz, 2 * GDN_HEADS, seq), F32)],
        scratch_shapes=[pltpu.VMEM((4, CARRY, d), F32),
                        pltpu.VMEM((4, CARRY + tm, IN_SLAB), F32),
                        pltpu.VMEM((tm, d), BF16)],
        compiler_params=pltpu.CompilerParams(
            dimension_semantics=("arbitrary", "arbitrary"),
            vmem_limit_bytes=_vmem_limit(
                resident=[((d, W_ALL_COLS), BF16), ((d, d), BF16)],
                pipelined=[((tm, d), F32), ((tm, tm), BF16), ((tm, AB_PAD), F32), ((2 * GDN_HEADS, tm), F32)]
                + 6 * [((tm, d), BF16)],
                scratch=[((4, CARRY, d), F32), ((4, CARRY + tm, IN_SLAB), F32), ((tm, d), BF16)])),
        name="in_proj",
    )(x, mod3, norm1_w, w_all, conv_w, sc_conv_w, alog_row, dtb_row, w_sc_out, tri)


def _block_diag(x, left):
    zero = jnp.zeros_like(x)
    return jnp.concatenate([jnp.where(left, x, zero), jnp.where(left, zero, x)], axis=1)


def _tri_inverse_stages(low, left, blk16, blk32, eye):
    def mm(x, y):
        return _bmm(x, _block_diag(y.astype(BF16), left))

    a1 = jnp.where(blk16, -low, 0.0)
    p = eye + a1
    sq = mm(a1, a1)
    yield
    for _ in range(2):
        both = mm(jnp.concatenate([p, sq], axis=1), sq)
        yield
        p = p + both[:, 0:CHUNK]
        sq = both[:, CHUNK:2 * CHUNK]
    p = p + mm(p, sq)
    yield
    for off in (jnp.where(jnp.logical_and(blk32, jnp.logical_not(blk16)), low, 0.0),
                jnp.where(blk32, 0.0, low)):
        po = mm(p, off)
        yield
        p = p - mm(po, p)
        yield
    return p


def _gdn_kernel(qn_ref, kn_ref, vn_ref, gbc_ref, gbr_ref,
                qp_ref, kp_ref, zs_ref, ga_ref, ybg_ref, x_ref, mod_ref, nw_ref, wg_ref, wo_ref,
                out_ref, state_ref, u_scr, w_scr, qk_scr, ab_scr, eb_scr, dl_scr):
    i = pl.program_id(0)
    tm = TM_GDN
    ns = GDN_SEQS
    nh = GDN_HEADS
    nb = ns * nh
    d = D_MODEL
    slot_w = lax.rem(i, 2)
    slot_r = 1 - slot_w
    t_fin = lax.rem(jnp.maximum(i - 1, 0), GDN_TILES)

    @pl.when(i == 0)
    def _():
        u_scr[1] = jnp.zeros(u_scr.shape[1:], u_scr.dtype)
        w_scr[1] = jnp.zeros(w_scr.shape[1:], w_scr.dtype)
        qk_scr[1] = jnp.zeros(qk_scr.shape[1:], qk_scr.dtype)
        ab_scr[1] = jnp.zeros(ab_scr.shape[1:], ab_scr.dtype)
        eb_scr[1] = jnp.zeros(eb_scr.shape[1:], eb_scr.dtype)
        dl_scr[1] = jnp.zeros(dl_scr.shape[1:], dl_scr.dtype)

    @pl.when(t_fin == 0)
    def _():
        state_ref[...] = jnp.zeros_like(state_ref)

    def prepare():
        gbc = gbc_ref[...]
        gbr = gbr_ref[...]

        def col_heads(j0):
            return jnp.stack([jnp.broadcast_to(gbc[sq, :, j0 + h:j0 + h + 1], (tm, HEAD_DIM))
                              for sq in range(ns) for h in range(nh)])

        def row_heads(j0):
            return jnp.stack([jnp.broadcast_to(gbr[sq, j0 + h:j0 + h + 1, :], (CHUNK, tm))
                              for sq in range(ns) for h in range(nh)])

        ri = lax.broadcasted_iota(jnp.int32, (1, CHUNK, tm), 1)
        li = lax.broadcasted_iota(jnp.int32, (1, CHUNK, tm), 2)
        left = li < CHUNK
        ci = jnp.where(left, li, li - CHUNK)
        causal = ri >= ci
        strict = ri > ci
        blk16 = (ri // 16) == (ci // 16)
        blk32 = (ri // 32) == (ci // 32)
        eye = jnp.where(ri == ci, 1.0, 0.0).astype(F32)

        def side_by_side(g):
            return jnp.where(left, g[:, 0:CHUNK, :], g[:, CHUNK:tm, :])

        q = qn_ref[...].reshape(nb, tm, HEAD_DIM)
        k = kn_ref[...].reshape(nb, tm, HEAD_DIM)
        v = vn_ref[...].reshape(nb, tm, HEAD_DIM)

        gkk = side_by_side(_bmm_nt(k, k))
        yield
        gqk = side_by_side(_bmm_nt(q, k))
        yield

        gcb = col_heads(0)
        ab = jnp.exp(gcb)
        gc2 = gcb.reshape(nb * 2, CHUNK, HEAD_DIM)
        ab_scr[slot_w] = ab.astype(ab_scr.dtype)
        eb_scr[slot_w] = jnp.exp(gc2[:, CHUNK - 1:CHUNK, :] - gc2).reshape(nb, tm, HEAD_DIM).astype(eb_scr.dtype)
        dl_scr[slot_w] = ab.reshape(nb, 2, CHUNK, HEAD_DIM)[:, :, CHUNK - 1, :]
        grow = row_heads(0)
        brow = row_heads(nh)
        barow = brow * jnp.exp(grow)

        dec = jnp.where(causal, jnp.exp(jnp.minimum(side_by_side(gcb) - grow, 0.0)), 0.0)
        bcol = side_by_side(col_heads(nh))
        low = jnp.where(strict, gkk * bcol * dec, 0.0)
        qk = jnp.where(causal, gqk * dec, 0.0).astype(qk_scr.dtype)
        qk_scr[slot_w, 0] = qk[:, :, 0:CHUNK]
        qk_scr[slot_w, 1] = qk[:, :, CHUNK:tm]
        tinv = yield from _tri_inverse_stages(low, left, blk16, blk32, eye)
        u_scr[slot_w] = _bmm(_block_diag((tinv * brow).astype(BF16), left), v)
        yield
        w_scr[slot_w] = _bmm(_block_diag((tinv * barow).astype(BF16), left), k).astype(w_scr.dtype)
        yield

    def finish():
        q = qp_ref[...].reshape(nb, tm, HEAD_DIM)
        k = kp_ref[...].reshape(nb, tm, HEAD_DIM)
        s = state_ref[...]
        outs = []
        for c in range(tm // CHUNK):
            rows = slice(c * CHUNK, (c + 1) * CHUNK)
            ws_qs = _bmm(jnp.concatenate([w_scr[slot_r, :, rows, :], q[:, rows]], axis=1), s)
            yield
            v_new = u_scr[slot_r, :, rows, :] - ws_qs[:, 0:CHUNK]
            outs.append(ab_scr[slot_r, :, rows, :].astype(F32) * ws_qs[:, CHUNK:]
                        + _bmm(qk_scr[slot_r, c], v_new))
            yield
            s = (s * dl_scr[slot_r, :, c:c + 1, :]
                 + _bmm_tn(k[:, rows], eb_scr[slot_r, :, rows, :].astype(F32) * v_new))
            yield
        state_ref[...] = s

        o = jnp.concatenate(outs, axis=1)
        on = (_rms(o) * nw_ref[...] * zs_ref[...].reshape(nb, tm, HEAD_DIM).astype(F32)).astype(BF16)
        o2d = jnp.concatenate([jnp.concatenate([on[sq * nh + h] for h in range(nh)], axis=1)
                               for sq in range(ns)], axis=0)
        ga = ga_ref[...].reshape(ns * tm, d)
        ybg = ybg_ref[...].reshape(ns * tm, d)
        merged = []
        for c0 in range(0, d, GDN_SLAB):
            ya = jnp.dot(o2d, wg_ref[:, c0:c0 + GDN_SLAB], preferred_element_type=F32)
            merged.append((ga[:, c0:c0 + GDN_SLAB].astype(F32) * ya
                           + ybg[:, c0:c0 + GDN_SLAB].astype(F32)).astype(BF16))
            yield
        merged = jnp.concatenate(merged, axis=1)
        for c0 in range(0, d, GDN_SLAB):
            mix = jnp.dot(merged, wo_ref[:, c0:c0 + GDN_SLAB], preferred_element_type=F32)
            for sq in range(ns):
                out_ref[sq, :, c0:c0 + GDN_SLAB] = (x_ref[sq, :, c0:c0 + GDN_SLAB]
                                                    + mod_ref[sq, 2:3, c0:c0 + GDN_SLAB] * mix[sq * tm:(sq + 1) * tm])
            yield

    active = [finish(), prepare()]
    while active:
        for stage in list(active):
            try:
                next(stage)
            except StopIteration:
                active.remove(stage)


def _gdn_call(q, k, v, zs, gbc, gbr, ga, ybg, x, mod3, gdn_norm_w, wg, wo):
    bsz, seq, d = x.shape
    tm = TM_GDN
    ns = GDN_SEQS
    nb = ns * GDN_HEADS
    assert seq // tm == GDN_TILES and tm == 2 * CHUNK and bsz % ns == 0
    last = (bsz // ns) * GDN_TILES - 1

    def tile(j):
        return j // GDN_TILES, lax.rem(j, GDN_TILES)

    def nxt(i):
        return tile(jnp.minimum(i, last))

    def fin(i):
        return tile(jnp.maximum(i - 1, 0))

    const2 = lambda i: (0, 0)
    resident = dict(pipeline_mode=pl.Buffered(1))
    head_blk = (ns, GDN_HEADS, tm, HEAD_DIM)
    head_nxt = pl.BlockSpec(head_blk, lambda i: (nxt(i)[0], 0, nxt(i)[1], 0))
    head_fin = pl.BlockSpec(head_blk, lambda i: (fin(i)[0], 0, fin(i)[1], 0))
    tok_fin = pl.BlockSpec((ns, tm, d), lambda i: (fin(i)[0], fin(i)[1], 0))
    scratch = [((nb, HEAD_DIM, HEAD_DIM), F32),
               ((2, nb, tm, HEAD_DIM), F32),
               ((2, nb, tm, HEAD_DIM), BF16),
               ((2, 2, nb, CHUNK, CHUNK), BF16),
               ((2, nb, tm, HEAD_DIM), BF16),
               ((2, nb, tm, HEAD_DIM), BF16),
               ((2, nb, 2, HEAD_DIM), F32)]
    return pl.pallas_call(
        _gdn_kernel,
        grid=(last + 2,),
        in_specs=[head_nxt, head_nxt, head_nxt,
                  pl.BlockSpec((ns, tm, AB_PAD), lambda i: (nxt(i)[0], nxt(i)[1], 0)),
                  pl.BlockSpec((ns, 2 * GDN_HEADS, tm), lambda i: (nxt(i)[0], 0, nxt(i)[1])),
                  head_fin, head_fin, head_fin, tok_fin, tok_fin, tok_fin,
                  pl.BlockSpec((ns, N_MOD, d), lambda i: (fin(i)[0], 0, 0)),
                  pl.BlockSpec((1, HEAD_DIM), const2),
                  pl.BlockSpec((d, d), const2, **resident),
                  pl.BlockSpec((d, d), const2, **resident)],
        out_specs=tok_fin,
        out_shape=jax.ShapeDtypeStruct((bsz, seq, d), F32),
        scratch_shapes=[pltpu.VMEM(*s) for s in scratch],
        compiler_params=pltpu.CompilerParams(
            dimension_semantics=("arbitrary",),
            vmem_limit_bytes=_vmem_limit(
                resident=[((d, d), BF16), ((d, d), BF16)],
                pipelined=6 * [(head_blk, BF16)] + 2 * [((ns, tm, d), BF16)] + 2 * [((ns, tm, d), F32)]
                + [((ns, tm, AB_PAD), F32), ((ns, 2 * GDN_HEADS, tm), F32), ((ns, 8, d), F32)],
                scratch=scratch)),
        name="gdn_mix",
    )(q, k, v, gbc, gbr, q, k, zs, ga, ybg, x, mod3, gdn_norm_w, wg, wo)


def _ffn_kernel(x_ref, mod_ref, modf_ref, n2_ref, nf_ref, win_ref, wout_ref, out_ref):
    x = x_ref[0]
    sh2 = mod_ref[0, 3:4, :]
    sc2 = mod_ref[0, 4:5, :]
    g2 = mod_ref[0, 5:6, :]
    h = (_rms(x) * n2_ref[...] * (1.0 + sc2) + sh2).astype(BF16)
    acts = []
    for s0 in range(0, D_FF, FF_SLAB):
        gate = jnp.dot(h, win_ref[:, s0:s0 + FF_SLAB], preferred_element_type=F32)
        up = jnp.dot(h, win_ref[:, D_FF + s0:D_FF + s0 + FF_SLAB], preferred_element_type=F32)
        acts.append((_silu(gate) * up).astype(BF16))
    ffn = jnp.dot(jnp.concatenate(acts, axis=1), wout_ref[...], preferred_element_type=F32)
    x2 = x + g2 * ffn
    out_ref[0] = _rms(x2) * nf_ref[...] * (1.0 + modf_ref[0, 1:2, :]) + modf_ref[0, 0:1, :]


def _ffn_call(x, mod3, modf3, norm2_w, normf_w, w_in, w_out):
    bsz, seq, d = x.shape
    tm = TM_FFN
    const2 = lambda b, t: (0, 0)
    resident = dict(pipeline_mode=pl.Buffered(1))
    tok_spec = pl.BlockSpec((1, tm, d), lambda b, t: (b, t, 0))
    return pl.pallas_call(
        _ffn_kernel,
        grid=(bsz, seq // tm),
        in_specs=[tok_spec,
                  pl.BlockSpec((1, N_MOD, d), lambda b, t: (b, 0, 0)),
                  pl.BlockSpec((1, 2, d), lambda b, t: (b, 0, 0)),
                  pl.BlockSpec((1, d), const2),
                  pl.BlockSpec((1, d), const2),
                  pl.BlockSpec((d, 2 * D_FF), const2, **resident),
                  pl.BlockSpec((D_FF, d), const2, **resident)],
        out_specs=tok_spec,
        out_shape=jax.ShapeDtypeStruct((bsz, seq, d), F32),
        compiler_params=pltpu.CompilerParams(
            dimension_semantics=("arbitrary", "arbitrary"),
            vmem_limit_bytes=_vmem_limit(resident=[((d, 2 * D_FF), BF16), ((D_FF, d), BF16)],
                                         pipelined=2 * [((tm, d), F32)] + [((8, d), F32), ((8, d), F32)],
                                         scratch=[((tm, D_FF), BF16)])),
        name="ffn_final",
    )(x, mod3, modf3, norm2_w, normf_w, w_in, w_out)


def kernel(x, c, w_ada, b_ada, norm1_w, w_in, gdn_conv_w, gdn_a_log, gdn_dt_bias, gdn_norm_w, w_gdn_proj,
           sc_conv_w, w_sc_out, w_o, norm2_w, w_ffn_in, w_ffn_out, w_ada_f, b_ada_f, normf_w):
    bsz, seq, d = x.shape
    assert w_ada.shape[0] == 1 and d == D_MODEL and seq % TM_FFN == 0

    mod3 = _mod_call(c, w_ada[0], b_ada[0]).reshape(bsz, N_MOD, d)
    modf3 = _mod_call(c, w_ada_f, b_ada_f).reshape(bsz, 2, d)

    w_all = _repack_call(jnp.transpose(w_in[0]))
    pad_row = lambda v: jnp.pad(v, (0, AB_PAD - GDN_HEADS)).reshape(1, AB_PAD)
    ii = jnp.arange(TM_IN)
    tri = ((ii[:, None] >= ii[None, :]) & ((ii[:, None] // CHUNK) == (ii[None, :] // CHUNK))).astype(BF16)

    q, k, v, zs, ga, ybg, gbc, gbr = _in_call(
        x, mod3, norm1_w, w_all, gdn_conv_w[0], sc_conv_w[0], pad_row(gdn_a_log[0]), pad_row(gdn_dt_bias[0]),
        w_sc_out[0].astype(BF16), tri)

    wg = w_gdn_proj[0].astype(BF16)
    x1 = _gdn_call(q, k, v, zs, gbc, gbr, ga, ybg, x, mod3, gdn_norm_w, wg, w_o[0].astype(BF16))

    return _ffn_call(x1, mod3, modf3, norm2_w, normf_w.reshape(1, d), w_ffn_in[0].astype(BF16), w_ffn_out[0].astype(BF16))
```

```python
import jax
import jax.numpy as jnp
from jax import lax
from jax.experimental import pallas as pl
from jax.experimental.pallas import tpu as pltpu

D_MODEL = 1024
GDN_HEADS = 8
HEAD_DIM = 128
GDN_CONV = 4
SC_CONV = 3
CHUNK = 64
EPS = 1e-6
N_MOD = 6
D_FF = 2816
AB_PAD = 128

C_Q, C_K, C_V, C_Z, C_SCB, C_SCC, C_SCX, C_GA, C_GB, C_AB = (
    0, 1024, 2048, 3072, 4096, 5120, 6144, 7168, 8192, 9216)
W_ALL_COLS = C_AB + AB_PAD

TM_IN = 256
TM_GDN = 128
GDN_SEQS = 4
GDN_TILES = 16
GDN_SLAB = 512
TM_FFN = 1024
FF_SLAB = 256
IN_SLAB = 256
CARRY = 8
REPACK_SLAB = 512

BF16 = jnp.bfloat16
F32 = jnp.float32
MIB = 1024 * 1024
VMEM_HEADROOM = 1.33


def _nbytes(shape, dtype):
    n = jnp.dtype(dtype).itemsize
    for s in shape:
        n *= s
    return n


def _vmem_limit(resident=(), pipelined=(), scratch=()):
    total = (sum(_nbytes(*b) for b in resident) + 2 * sum(_nbytes(*b) for b in pipelined)
             + sum(_nbytes(*b) for b in scratch))
    return int(-(-total * VMEM_HEADROOM // MIB)) * MIB


def _mm(a, b):
    return jnp.dot(a.astype(BF16), b.astype(BF16), preferred_element_type=F32)


def _bmm(a, b):
    return lax.dot_general(a.astype(BF16), b.astype(BF16), (((2,), (1,)), ((0,), (0,))),
                           preferred_element_type=F32)


def _bmm_nt(a, b):
    return lax.dot_general(a.astype(BF16), b.astype(BF16), (((2,), (2,)), ((0,), (0,))),
                           preferred_element_type=F32)


def _bmm_tn(a, b):
    return lax.dot_general(a.astype(BF16), b.astype(BF16), (((1,), (1,)), ((0,), (0,))),
                           preferred_element_type=F32)


def _silu(x):
    return x * jax.nn.sigmoid(x)


def _rms(x):
    return x * lax.rsqrt(jnp.mean(x * x, axis=-1, keepdims=True) + EPS)


def _mod_kernel(c_ref, w_ref, b_ref, o_ref):
    o_ref[...] = _mm(_silu(c_ref[...]), w_ref[...]) + b_ref[...]


def _mod_call(c, w, b):
    bsz, d = c.shape
    n = w.shape[1]
    tn = 1024
    return pl.pallas_call(
        _mod_kernel,
        grid=(n // tn,),
        in_specs=[pl.BlockSpec((bsz, d), lambda j: (0, 0)),
                  pl.BlockSpec((d, tn), lambda j: (0, j)),
                  pl.BlockSpec((1, tn), lambda j: (0, j))],
        out_specs=pl.BlockSpec((bsz, tn), lambda j: (0, j)),
        out_shape=jax.ShapeDtypeStruct((bsz, n), F32),
        compiler_params=pltpu.CompilerParams(
            dimension_semantics=("arbitrary",),
            vmem_limit_bytes=_vmem_limit(pipelined=[((bsz, d), F32), ((d, tn), F32), ((8, tn), F32), ((bsz, tn), F32)])),
        name="mod_rows",
    )(c, w, b.reshape(1, n))


def _repack_kernel(wt_ref, ab_ref, o_ref):
    j = pl.program_id(0)

    @pl.when(j < C_AB // REPACK_SLAB)
    def _():
        o_ref[...] = jnp.transpose(wt_ref[...]).astype(BF16)

    @pl.when(j == C_AB // REPACK_SLAB)
    def _():
        o_ref[...] = jnp.zeros(o_ref.shape, BF16)
        o_ref[:, 0:2 * GDN_HEADS] = jnp.transpose(ab_ref[...]).astype(BF16)


def _repack_call(wt):
    n, d = wt.shape
    nab = 2 * GDN_HEADS
    assert n == C_AB + nab
    sl = REPACK_SLAB
    n_main = C_AB // sl
    n_head = C_SCB // sl

    def src_row(j):
        jj = jnp.minimum(j, n_main - 1)
        return (jj * (sl // nab) + jnp.where(jj < n_head, 0, 1)) * nab

    return pl.pallas_call(
        _repack_kernel,
        grid=(n_main + 1,),
        in_specs=[pl.BlockSpec((pl.Element(sl), pl.Element(d)), lambda j: (src_row(j), 0)),
                  pl.BlockSpec((pl.Element(nab), pl.Element(d)), lambda j: (C_SCB, 0))],
        out_specs=pl.BlockSpec((d, sl), lambda j: (0, j)),
        out_shape=jax.ShapeDtypeStruct((d, W_ALL_COLS), BF16),
        compiler_params=pltpu.CompilerParams(
            dimension_semantics=("arbitrary",),
            vmem_limit_bytes=_vmem_limit(pipelined=[((sl, d), F32), ((nab, d), F32), ((d, sl), BF16)],
                                         scratch=[((d, sl), F32)])),
        name="repack_w_in",
    )(wt, wt)


def _in_kernel(x_ref, mod_ref, n1_ref, w_ref, cw_ref, scw_ref, alog_ref, dtb_ref, wsc_ref, tri_ref,
               q_ref, k_ref, v_ref, zs_ref, ga_ref, ybg_ref, gbc_ref, gbr_ref, carry, pbuf, ubuf):
    t = pl.program_id(1)
    tm = TM_IN

    @pl.when(t == 0)
    def _():
        carry[...] = jnp.zeros_like(carry)

    x = x_ref[0]
    sh1 = mod_ref[0, 0:1, :]
    sc1 = mod_ref[0, 1:2, :]
    h = (_rms(x) * n1_ref[...] * (1.0 + sc1) + sh1).astype(BF16)

    sw = IN_SLAB
    heads_per_slab = sw // HEAD_DIM

    def proj(c0, width=IN_SLAB):
        return jnp.dot(h, w_ref[:, c0:c0 + width], preferred_element_type=F32)

    def causal_conv(slot, col, p, taps, width):
        pbuf[slot, 0:CARRY, :] = carry[slot, :, col:col + sw]
        pbuf[slot, CARRY:CARRY + tm, :] = p
        acc = taps[width - 1:width, :] * p
        for j in range(width - 1):
            sh = width - 1 - j
            acc = acc + taps[j:j + 1, :] * pbuf[slot, CARRY - sh:CARRY - sh + tm, :]
        carry[slot, :, col:col + sw] = p[tm - CARRY:tm]
        return acc

    def gdn_input(slot, c0, col, out_ref, norm, scale):
        taps = cw_ref[:, c0 + col:c0 + col + sw]
        y = _silu(causal_conv(slot, col, proj(c0 + col), taps, GDN_CONV))
        for i in range(heads_per_slab):
            yh = y[:, i * HEAD_DIM:(i + 1) * HEAD_DIM]
            if norm:
                yh = yh * lax.rsqrt(jnp.sum(yh * yh, axis=-1, keepdims=True) + EPS)
                if scale != 1.0:
                    yh = yh * scale
            out_ref[0, col // HEAD_DIM + i] = yh.astype(out_ref.dtype)

    def z_gate(col):
        zs = _silu(proj(C_Z + col))
        for i in range(heads_per_slab):
            zs_ref[0, col // HEAD_DIM + i] = zs[:, i * HEAD_DIM:(i + 1) * HEAD_DIM].astype(zs_ref.dtype)

    def decay_beta():
        pab = proj(C_AB, AB_PAD)
        lane = lax.broadcasted_iota(jnp.int32, (tm, AB_PAD), 1)
        xg = pab + dtb_ref[...]
        softplus = jnp.maximum(xg, 0.0) + jnp.log1p(jnp.exp(-jnp.abs(xg)))
        g = -jnp.exp(alog_ref[...]) * softplus
        gb = jnp.where(lane < GDN_HEADS, g, jax.nn.sigmoid(pab))
        g1 = gb.astype(BF16)
        r1 = gb - g1.astype(F32)
        g2 = r1.astype(BF16)
        g3 = (r1 - g2.astype(F32)).astype(BF16)
        tri = tri_ref[...]
        cum = (jnp.dot(tri, g1, preferred_element_type=F32) + jnp.dot(tri, g2, preferred_element_type=F32)
               + jnp.dot(tri, g3, preferred_element_type=F32))
        gbc = jnp.where(lane < GDN_HEADS, cum, gb)
        gbc_ref[0] = gbc
        gbr_ref[0] = jnp.transpose(gbc)[0:2 * GDN_HEADS, :]

    def short_conv_in(col):
        prod = proj(C_SCC + col) * proj(C_SCX + col)
        conv = causal_conv(3, col, prod, scw_ref[:, col:col + sw], SC_CONV)
        ubuf[:, col:col + sw] = (proj(C_SCB + col) * conv).astype(BF16)

    def short_conv_out(col):
        yb = jnp.dot(ubuf[...], wsc_ref[:, col:col + sw], preferred_element_type=F32)
        ybg_ref[0, :, col:col + sw] = (jax.nn.sigmoid(proj(C_GB + col)) * yb).astype(ybg_ref.dtype)

    def gate_a(col):
        ga_ref[0, :, col:col + sw] = jax.nn.sigmoid(proj(C_GA + col)).astype(ga_ref.dtype)

    cols = range(0, D_MODEL, sw)
    for col in cols:
        gdn_input(0, C_Q, col, q_ref, True, HEAD_DIM ** -0.5)
        short_conv_in(col)
    for col in cols:
        gdn_input(1, C_K, col, k_ref, True, 1.0)
        gate_a(col)
        z_gate(col)
    for col in cols:
        gdn_input(2, C_V, col, v_ref, False, 1.0)
        short_conv_out(col)
    decay_beta()


def _in_call(x, mod3, norm1_w, w_all, conv_w, sc_conv_w, alog_row, dtb_row, w_sc_out, tri):
    bsz, seq, d = x.shape
    tm = TM_IN
    nt = seq // tm
    const2 = lambda b, t: (0, 0)
    resident = dict(pipeline_mode=pl.Buffered(1))
    head_spec = pl.BlockSpec((1, GDN_HEADS, tm, HEAD_DIM), lambda b, t: (b, 0, t, 0))
    tok_spec = pl.BlockSpec((1, tm, d), lambda b, t: (b, t, 0))
    head_shape = jax.ShapeDtypeStruct((bsz, GDN_HEADS, seq, HEAD_DIM), BF16)
    return pl.pallas_call(
        _in_kernel,
        grid=(bsz, nt),
        in_specs=[tok_spec,
                  pl.BlockSpec((1, N_MOD, d), lambda b, t: (b, 0, 0)),
                  pl.BlockSpec((1, d), const2),
                  pl.BlockSpec((d, W_ALL_COLS), const2, **resident),
                  pl.BlockSpec((GDN_CONV, 3 * d), const2),
                  pl.BlockSpec((SC_CONV, d), const2),
                  pl.BlockSpec((1, AB_PAD), const2),
                  pl.BlockSpec((1, AB_PAD), const2),
                  pl.BlockSpec((d, d), const2, **resident),
                  pl.BlockSpec((tm, tm), const2)],
        out_specs=[head_spec, head_spec, head_spec, head_spec, tok_spec, tok_spec,
                   pl.BlockSpec((1, tm, AB_PAD), lambda b, t: (b, t, 0)),
                   pl.BlockSpec((1, 2 * GDN_HEADS, tm), lambda b, t: (b, 0, t))],
        out_shape=[head_shape, head_shape, head_shape, head_shape,
                   jax.ShapeDtypeStruct((bsz, seq, d), BF16),
                   jax.ShapeDtypeStruct((bsz, seq, d), BF16),
                   jax.ShapeDtypeStruct((bsz, seq, AB_PAD), F32),
                   jax.ShapeDtypeStruct((bsz, 2 * GDN_HEADS, seq), F32)],
        scratch_shapes=[pltpu.VMEM((4, CARRY, d), F32),
                        pltpu.VMEM((4, CARRY + tm, IN_SLAB), F32),
                        pltpu.VMEM((tm, d), BF16)],
        compiler_params=pltpu.CompilerParams(
            dimension_semantics=("arbitrary", "arbitrary"),
            vmem_limit_bytes=_vmem_limit(
                resident=[((d, W_ALL_COLS), BF16), ((d, d), BF16)],
                pipelined=[((tm, d), F32), ((tm, tm), BF16), ((tm, AB_PAD), F32), ((2 * GDN_HEADS, tm), F32)]
                + 6 * [((tm, d), BF16)],
                scratch=[((4, CARRY, d), F32), ((4, CARRY + tm, IN_SLAB), F32), ((tm, d), BF16)])),
        name="in_proj",
    )(x, mod3, norm1_w, w_all, conv_w, sc_conv_w, alog_row, dtb_row, w_sc_out, tri)


def _block_diag(x, left):
    zero = jnp.zeros_like(x)
    return jnp.concatenate([jnp.where(left, x, zero), jnp.where(left, zero, x)], axis=1)


def _tri_inverse_stages(low, left, blk16, blk32, eye):
    def mm(x, y):
        return _bmm(x, _block_diag(y.astype(BF16), left))

    a1 = jnp.where(blk16, -low, 0.0)
    p = eye + a1
    sq = mm(a1, a1)
    yield
    for _ in range(2):
        both = mm(jnp.concatenate([p, sq], axis=1), sq)
        yield
        p = p + both[:, 0:CHUNK]
        sq = both[:, CHUNK:2 * CHUNK]
    p = p + mm(p, sq)
    yield
    for off in (jnp.where(jnp.logical_and(blk32, jnp.logical_not(blk16)), low, 0.0),
                jnp.where(blk32, 0.0, low)):
        po = mm(p, off)
        yield
        p = p - mm(po, p)
        yield
    return p


def _run(stages):
    try:
        while True:
            next(stages)
    except StopIteration as done:
        return done.value


def _gdn_kernel(qn_ref, kn_ref, vn_ref, gbc_ref, gbr_ref, zs_ref, ga_ref, ybg_ref, x_ref, mod_ref,
                nw_ref, wg_ref, wo_ref, out_ref, state_ref):
    tm = TM_GDN
    ns = GDN_SEQS
    nh = GDN_HEADS
    nb = ns * nh
    d = D_MODEL
    qp_ref, kp_ref = qn_ref, kn_ref

    @pl.when(pl.program_id(1) == 0)
    def _():
        state_ref[...] = jnp.zeros_like(state_ref)

    def prepare():
        gbc = gbc_ref[...]
        gbr = gbr_ref[...]

        def col_heads(j0):
            return jnp.stack([jnp.broadcast_to(gbc[sq, :, j0 + h:j0 + h + 1], (tm, HEAD_DIM))
                              for sq in range(ns) for h in range(nh)])

        def row_heads(j0):
            return jnp.stack([jnp.broadcast_to(gbr[sq, j0 + h:j0 + h + 1, :], (CHUNK, tm))
                              for sq in range(ns) for h in range(nh)])

        ri = lax.broadcasted_iota(jnp.int32, (1, CHUNK, tm), 1)
        li = lax.broadcasted_iota(jnp.int32, (1, CHUNK, tm), 2)
        left = li < CHUNK
        ci = jnp.where(left, li, li - CHUNK)
        causal = ri >= ci
        strict = ri > ci
        blk16 = (ri // 16) == (ci // 16)
        blk32 = (ri // 32) == (ci // 32)
        eye = jnp.where(ri == ci, 1.0, 0.0).astype(F32)

        def side_by_side(g):
            return jnp.where(left, g[:, 0:CHUNK, :], g[:, CHUNK:tm, :])

        q = qn_ref[...].reshape(nb, tm, HEAD_DIM)
        k = kn_ref[...].reshape(nb, tm, HEAD_DIM)
        v = vn_ref[...].reshape(nb, tm, HEAD_DIM)

        gkk = side_by_side(_bmm_nt(k, k))
        yield
        gqk = side_by_side(_bmm_nt(q, k))
        yield

        gcb = col_heads(0)
        ab = jnp.exp(gcb)
        gc2 = gcb.reshape(nb * 2, CHUNK, HEAD_DIM)
        eb = jnp.exp(gc2[:, CHUNK - 1:CHUNK, :] - gc2).reshape(nb, tm, HEAD_DIM)
        grow = row_heads(0)
        brow = row_heads(nh)
        barow = brow * jnp.exp(grow)

        dec = jnp.where(causal, jnp.exp(jnp.minimum(side_by_side(gcb) - grow, 0.0)), 0.0)
        bcol = side_by_side(col_heads(nh))
        low = jnp.where(strict, gkk * bcol * dec, 0.0)
        qk = jnp.where(causal, gqk * dec, 0.0).astype(BF16)
        tinv = yield from _tri_inverse_stages(low, left, blk16, blk32, eye)
        u = _bmm(_block_diag((tinv * brow).astype(BF16), left), v)
        yield
        w = _bmm(_block_diag((tinv * barow).astype(BF16), left), k).astype(BF16)
        yield
        return u, w, (qk[:, :, 0:CHUNK], qk[:, :, CHUNK:tm]), ab, eb

    def finish(u, w, qk, ab, eb):
        q = qp_ref[...].reshape(nb, tm, HEAD_DIM)
        k = kp_ref[...].reshape(nb, tm, HEAD_DIM)
        s = state_ref[...]
        outs = []
        for c in range(tm // CHUNK):
            rows = slice(c * CHUNK, (c + 1) * CHUNK)
            ws_qs = _bmm(jnp.concatenate([w[:, rows], q[:, rows]], axis=1), s)
            yield
            v_new = u[:, rows] - ws_qs[:, 0:CHUNK]
            outs.append(ab[:, rows] * ws_qs[:, CHUNK:] + _bmm(qk[c], v_new))
            yield
            s = (s * ab[:, (c + 1) * CHUNK - 1:(c + 1) * CHUNK]
                 + _bmm_tn(k[:, rows], eb[:, rows] * v_new))
            yield
        state_ref[...] = s

        o = jnp.concatenate(outs, axis=1)
        on = (_rms(o) * nw_ref[...] * zs_ref[...].reshape(nb, tm, HEAD_DIM).astype(F32)).astype(BF16)
        o2d = jnp.concatenate([jnp.concatenate([on[sq * nh + h] for h in range(nh)], axis=1)
                               for sq in range(ns)], axis=0)
        ga = ga_ref[...].reshape(ns * tm, d)
        ybg = ybg_ref[...].reshape(ns * tm, d)
        merged = []
        for c0 in range(0, d, GDN_SLAB):
            ya = jnp.dot(o2d, wg_ref[:, c0:c0 + GDN_SLAB], preferred_element_type=F32)
            merged.append((ga[:, c0:c0 + GDN_SLAB].astype(F32) * ya
                           + ybg[:, c0:c0 + GDN_SLAB].astype(F32)).astype(BF16))
            yield
        merged = jnp.concatenate(merged, axis=1)
        for c0 in range(0, d, GDN_SLAB):
            mix = jnp.dot(merged, wo_ref[:, c0:c0 + GDN_SLAB], preferred_element_type=F32)
            for sq in range(ns):
                out_ref[sq, :, c0:c0 + GDN_SLAB] = (x_ref[sq, :, c0:c0 + GDN_SLAB]
                                                    + mod_ref[sq, 2:3, c0:c0 + GDN_SLAB] * mix[sq * tm:(sq + 1) * tm])
            yield

    _run(finish(*_run(prepare())))


def _gdn_call(q, k, v, zs, gbc, gbr, ga, ybg, x, mod3, gdn_norm_w, wg, wo):
    bsz, seq, d = x.shape
    tm = TM_GDN
    ns = GDN_SEQS
    nb = ns * GDN_HEADS
    assert seq // tm == GDN_TILES and tm == 2 * CHUNK and bsz % ns == 0
    const2 = lambda b, t: (0, 0)
    resident = dict(pipeline_mode=pl.Buffered(1))
    head_blk = (ns, GDN_HEADS, tm, HEAD_DIM)
    head_spec = pl.BlockSpec(head_blk, lambda b, t: (b, 0, t, 0))
    tok_spec = pl.BlockSpec((ns, tm, d), lambda b, t: (b, t, 0))
    scratch = [((nb, HEAD_DIM, HEAD_DIM), F32)]
    temporaries = 4 * [((nb, tm, HEAD_DIM), F32)]
    return pl.pallas_call(
        _gdn_kernel,
        grid=(bsz // ns, GDN_TILES),
        in_specs=[head_spec, head_spec, head_spec,
                  pl.BlockSpec((ns, tm, AB_PAD), lambda b, t: (b, t, 0)),
                  pl.BlockSpec((ns, 2 * GDN_HEADS, tm), lambda b, t: (b, 0, t)),
                  head_spec, tok_spec, tok_spec, tok_spec,
                  pl.BlockSpec((ns, N_MOD, d), lambda b, t: (b, 0, 0)),
                  pl.BlockSpec((1, HEAD_DIM), const2),
                  pl.BlockSpec((d, d), const2, **resident),
                  pl.BlockSpec((d, d), const2, **resident)],
        out_specs=tok_spec,
        out_shape=jax.ShapeDtypeStruct((bsz, seq, d), F32),
        scratch_shapes=[pltpu.VMEM(*s) for s in scratch],
        compiler_params=pltpu.CompilerParams(
            dimension_semantics=("arbitrary", "arbitrary"),
            vmem_limit_bytes=_vmem_limit(
                resident=[((d, d), BF16), ((d, d), BF16)],
                pipelined=4 * [(head_blk, BF16)] + 2 * [((ns, tm, d), BF16)] + 2 * [((ns, tm, d), F32)]
                + [((ns, tm, AB_PAD), F32), ((ns, 2 * GDN_HEADS, tm), F32), ((ns, 8, d), F32)],
                scratch=scratch + temporaries)),
        name="gdn_mix",
    )(q, k, v, gbc, gbr, zs, ga, ybg, x, mod3, gdn_norm_w, wg, wo)


def _ffn_kernel(x_ref, mod_ref, modf_ref, n2_ref, nf_ref, win_ref, wout_ref, out_ref):
    x = x_ref[0]
    sh2 = mod_ref[0, 3:4, :]
    sc2 = mod_ref[0, 4:5, :]
    g2 = mod_ref[0, 5:6, :]
    h = (_rms(x) * n2_ref[...] * (1.0 + sc2) + sh2).astype(BF16)
    acts = []
    for s0 in range(0, D_FF, FF_SLAB):
        gate = jnp.dot(h, win_ref[:, s0:s0 + FF_SLAB], preferred_element_type=F32)
        up = jnp.dot(h, win_ref[:, D_FF + s0:D_FF + s0 + FF_SLAB], preferred_element_type=F32)
        acts.append((_silu(gate) * up).astype(BF16))
    ffn = jnp.dot(jnp.concatenate(acts, axis=1), wout_ref[...], preferred_element_type=F32)
    x2 = x + g2 * ffn
    out_ref[0] = _rms(x2) * nf_ref[...] * (1.0 + modf_ref[0, 1:2, :]) + modf_ref[0, 0:1, :]


def _ffn_call(x, mod3, modf3, norm2_w, normf_w, w_in, w_out):
    bsz, seq, d = x.shape
    tm = TM_FFN
    const2 = lambda b, t: (0, 0)
    resident = dict(pipeline_mode=pl.Buffered(1))
    tok_spec = pl.BlockSpec((1, tm, d), lambda b, t: (b, t, 0))
    return pl.pallas_call(
        _ffn_kernel,
        grid=(bsz, seq // tm),
        in_specs=[tok_spec,
                  pl.BlockSpec((1, N_MOD, d), lambda b, t: (b, 0, 0)),
                  pl.BlockSpec((1, 2, d), lambda b, t: (b, 0, 0)),
                  pl.BlockSpec((1, d), const2),
                  pl.BlockSpec((1, d), const2),
                  pl.BlockSpec((d, 2 * D_FF), const2, **resident),
                  pl.BlockSpec((D_FF, d), const2, **resident)],
        out_specs=tok_spec,
        out_shape=jax.ShapeDtypeStruct((bsz, seq, d), F32),
        compiler_params=pltpu.CompilerParams(
            dimension_semantics=("arbitrary", "arbitrary"),
            vmem_limit_bytes=_vmem_limit(resident=[((d, 2 * D_FF), BF16), ((D_FF, d), BF16)],
                                         pipelined=2 * [((tm, d), F32)] + [((8, d), F32), ((8, d), F32)],
                                         scratch=[((tm, D_FF), BF16)])),
        name="ffn_final",
    )(x, mod3, modf3, norm2_w, normf_w, w_in, w_out)


def kernel(x, c, w_ada, b_ada, norm1_w, w_in, gdn_conv_w, gdn_a_log, gdn_dt_bias, gdn_norm_w, w_gdn_proj,
           sc_conv_w, w_sc_out, w_o, norm2_w, w_ffn_in, w_ffn_out, w_ada_f, b_ada_f, normf_w):
    bsz, seq, d = x.shape
    assert w_ada.shape[0] == 1 and d == D_MODEL and seq % TM_FFN == 0

    mod3 = _mod_call(c, w_ada[0], b_ada[0]).reshape(bsz, N_MOD, d)
    modf3 = _mod_call(c, w_ada_f, b_ada_f).reshape(bsz, 2, d)

    w_all = _repack_call(jnp.transpose(w_in[0]))
    pad_row = lambda v: jnp.pad(v, (0, AB_PAD - GDN_HEADS)).reshape(1, AB_PAD)
    ii = jnp.arange(TM_IN)
    tri = ((ii[:, None] >= ii[None, :]) & ((ii[:, None] // CHUNK) == (ii[None, :] // CHUNK))).astype(BF16)

    q, k, v, zs, ga, ybg, gbc, gbr = _in_call(
        x, mod3, norm1_w, w_all, gdn_conv_w[0], sc_conv_w[0], pad_row(gdn_a_log[0]), pad_row(gdn_dt_bias[0]),
        w_sc_out[0].astype(BF16), tri)

    wg = w_gdn_proj[0].astype(BF16)
    x1 = _gdn_call(q, k, v, zs, gbc, gbr, ga, ybg, x, mod3, gdn_norm_w, wg, w_o[0].astype(BF16))

    return _ffn_call(x1, mod3, modf3, norm2_w, normf_w.reshape(1, d), w_ffn_in[0].astype(BF16), w_ffn_out[0].astype(BF16))
```
